```python
import jax, jax.numpy as jnp
from jax import lax
import numpy as np

D_MODEL = 1024
BATCH = 2
SEQ = 8192
DEPTH = 2

CHUNK = 64
N_MIXERS = 2
N_A = (DEPTH + 1) // 2
N_B = DEPTH // 2
FOX_HEADS = 16
FOX_HEAD_DIM = D_MODEL // FOX_HEADS
Q_BLOCK = 128
RNN_WIDTH = (4 * D_MODEL // 3) // 128 * 128
RNN_BLOCKS = 16
RNN_BLOCK_W = RNN_WIDTH // RNN_BLOCKS
CONV_WIDTH = 4
RG_C = 8.0
D_FF = (8 * D_MODEL // 3 + 255) // 256 * 256
RMS_EPS = 1e-6

kernel_name = "fox_rglru_macaron_hybrid"


def rmsnorm(x, g):
    xf = x.astype(jnp.float32)
    y = xf * lax.rsqrt(jnp.mean(xf * xf, axis=-1, keepdims=True) + RMS_EPS)
    return (y * g.astype(jnp.float32)).astype(x.dtype)


def swiglu(h, w_gu, w_down):
    g, u = jnp.split(h @ w_gu, 2, axis=-1)
    return (jax.nn.silu(g) * u) @ w_down


def forgetting_attention_mixer(h, w_in, b_f, w_out):
    B, T, _ = h.shape
    proj = h @ w_in
    q, k, v, f_logit = jnp.split(proj, [D_MODEL, 2 * D_MODEL, 3 * D_MODEL], axis=-1)
    q = q.reshape(B, T, FOX_HEADS, FOX_HEAD_DIM)
    k = k.reshape(B, T, FOX_HEADS, FOX_HEAD_DIM)
    v = v.reshape(B, T, FOX_HEADS, FOX_HEAD_DIM)
    log_f = jax.nn.log_sigmoid(f_logit.astype(jnp.float32) + b_f.astype(jnp.float32))
    F = jnp.transpose(jnp.cumsum(log_f, axis=1), (0, 2, 1))
    scale = FOX_HEAD_DIM ** -0.5
    k_pos = jnp.arange(T)

    def q_block(i):
        start = i * Q_BLOCK
        qb = lax.dynamic_slice_in_dim(q, start, Q_BLOCK, axis=1)
        Fq = lax.dynamic_slice_in_dim(F, start, Q_BLOCK, axis=2)
        s = jnp.einsum('bqhd,bkhd->bhqk', qb, k, preferred_element_type=jnp.float32) * scale
        s = s + Fq[..., :, None] - F[..., None, :]
        q_pos = start + jnp.arange(Q_BLOCK)
        s = jnp.where(k_pos[None, :] <= q_pos[:, None], s, -jnp.inf)
        p = jax.nn.softmax(s, axis=-1)
        return jnp.einsum('bhqk,bkhd->bqhd', p.astype(v.dtype), v)

    out = lax.map(q_block, jnp.arange(T // Q_BLOCK))
    out = jnp.moveaxis(out, 0, 1).reshape(B, T, D_MODEL)
    return out @ w_out


def _lru_combine(c1, c2):
    a1, b1 = c1
    a2, b2 = c2
    return a1 * a2, a2 * b1 + b2


def rglru_mixer(h, w_in, conv_w, conv_b, w_a, b_a, w_i, b_i, lam, w_out):
    B, T, _ = h.shape
    gate_br, rec_br = jnp.split(h @ w_in, 2, axis=-1)
    gate_br = jax.nn.gelu(gate_br)
    xp = jnp.pad(rec_br, ((0, 0), (CONV_WIDTH - 1, 0), (0, 0)))
    xc = conv_b + xp[:, 0:T] * conv_w[0]
    for j in range(1, CONV_WIDTH):
        xc = xc + xp[:, j:j + T] * conv_w[j]
    xb = xc.reshape(B, T, RNN_BLOCKS, RNN_BLOCK_W)
    r = jax.nn.sigmoid((jnp.einsum('btnc,ncd->btnd', xb, w_a).reshape(B, T, RNN_WIDTH) + b_a).astype(jnp.float32))
    gi = jax.nn.sigmoid((jnp.einsum('btnc,ncd->btnd', xb, w_i).reshape(B, T, RNN_WIDTH) + b_i).astype(jnp.float32))
    log_a = -RG_C * r * jax.nn.softplus(-lam.astype(jnp.float32))
    a = jnp.exp(log_a)
    b = jnp.sqrt(-jnp.expm1(2.0 * log_a)) * (gi * xc.astype(jnp.float32))
    _, hs = lax.associative_scan(_lru_combine, (a, b), axis=1)
    y = hs.astype(h.dtype) * gate_br
    return y @ w_out


def _normal(key, shape, fan_in):
    return jax.random.normal(key, shape, jnp.float32) * (fan_in ** -0.5)


def _gain(key, shape):
    return 1.0 + 0.05 * jax.random.normal(key, shape, jnp.float32)


def setup_inputs(seed: int = 0) -> dict:
    key = jax.random.key(seed)
    ks = jax.random.split(key, 24)
    D, R = D_MODEL, RNN_WIDTH
    a0 = jax.random.uniform(ks[20], (N_B, R), jnp.float32, 0.9, 0.999)
    s0 = a0 ** (1.0 / RG_C)
    return {
        "x": jax.random.normal(ks[0], (BATCH, SEQ, D), jnp.float32),
        "ffn1_norm": _gain(ks[1], (DEPTH, D)),
        "ffn1_w_gu": _normal(ks[2], (DEPTH, D, 2 * D_FF), D),
        "ffn1_w_down": _normal(ks[3], (DEPTH, D_FF, D), D_FF),
        "mix_norm": _gain(ks[4], (DEPTH, D)),
        "ffn2_norm": _gain(ks[5], (DEPTH, D)),
        "ffn2_w_gu": _normal(ks[6], (DEPTH, D, 2 * D_FF), D),
        "ffn2_w_down": _normal(ks[7], (DEPTH, D_FF, D), D_FF),
        "fox_w_in": _normal(ks[8], (N_A, D, 3 * D + FOX_HEADS), D),
        "fox_b_f": 2.0 + 0.5 * jax.random.normal(ks[9], (N_A, FOX_HEADS), jnp.float32),
        "fox_w_out": _normal(ks[10], (N_A, D, D), D),
        "lru_w_in": _normal(ks[11], (N_B, D, 2 * R), D),
        "lru_conv_w": _normal(ks[12], (N_B, CONV_WIDTH, R), CONV_WIDTH),
        "lru_conv_b": 0.02 * jax.random.normal(ks[13], (N_B, R), jnp.float32),
        "lru_w_a": _normal(ks[14], (N_B, RNN_BLOCKS, RNN_BLOCK_W, RNN_BLOCK_W), RNN_BLOCK_W),
        "lru_b_a": 0.02 * jax.random.normal(ks[15], (N_B, R), jnp.float32),
        "lru_w_i": _normal(ks[16], (N_B, RNN_BLOCKS, RNN_BLOCK_W, RNN_BLOCK_W), RNN_BLOCK_W),
        "lru_b_i": 0.02 * jax.random.normal(ks[17], (N_B, R), jnp.float32),
        "lru_lambda": jnp.log(s0) - jnp.log1p(-s0),
        "lru_w_out": _normal(ks[18], (N_B, R, D), R),
        "final_norm": _gain(ks[19], (D,)),
    }


def reference(x, ffn1_norm, ffn1_w_gu, ffn1_w_down, mix_norm, ffn2_norm, ffn2_w_gu, ffn2_w_down,
              fox_w_in, fox_b_f, fox_w_out, lru_w_in, lru_conv_w, lru_conv_b, lru_w_a, lru_b_a,
              lru_w_i, lru_b_i, lru_lambda, lru_w_out, final_norm):
    for i in range(DEPTH):
        x = x + 0.5 * swiglu(rmsnorm(x, ffn1_norm[i]), ffn1_w_gu[i], ffn1_w_down[i])
        h = rmsnorm(x, mix_norm[i])
        j = i // N_MIXERS
        if i % N_MIXERS == 0:
            m = forgetting_attention_mixer(h, fox_w_in[j], fox_b_f[j], fox_w_out[j])
        else:
            m = rglru_mixer(h, lru_w_in[j], lru_conv_w[j], lru_conv_b[j], lru_w_a[j], lru_b_a[j],
                            lru_w_i[j], lru_b_i[j], lru_lambda[j], lru_w_out[j])
        x = x + m
        x = x + 0.5 * swiglu(rmsnorm(x, ffn2_norm[i]), ffn2_w_gu[i], ffn2_w_down[i])
    return rmsnorm(x, final_norm)
```

```python
import functools

import jax
import jax.numpy as jnp
import numpy as np
from jax import lax
from jax.experimental import pallas as pl
from jax.experimental.pallas import tpu as pltpu

D_MODEL = 1024
DEPTH = 2
N_MIXERS = 2
FOX_HEADS = 16
FOX_HEAD_DIM = D_MODEL // FOX_HEADS
RNN_WIDTH = 1280
RNN_BLOCKS = 16
RNN_BLOCK_W = RNN_WIDTH // RNN_BLOCKS
CONV_WIDTH = 4
RG_C = 8.0
D_FF = 2816
RMS_EPS = 1e-6

LANES = 128
SUBLANES = 8
VMEM_LIMIT_BYTES = 56 * 1024 * 1024

FFN_ROWS = 512
FFN_CHUNKS = 2
PROJ_ROWS = 512
ATTN_BLOCK = 256
LRU_ROWS = 256
LRU_GROUPS = 2
LRU_GROUP_W = RNN_WIDTH // LRU_GROUPS

BF16 = jnp.bfloat16
F32 = jnp.float32


def _params(*sem):
    return pltpu.CompilerParams(dimension_semantics=sem, vmem_limit_bytes=VMEM_LIMIT_BYTES)


def _resident(shape):
    nd = len(shape)
    return pl.BlockSpec(shape, lambda *_: (0,) * nd, pipeline_mode=pl.Buffered(1))


def _rmsnorm(x, g):
    ms = jnp.mean(x * x, axis=-1, keepdims=True)
    return x * lax.rsqrt(ms + RMS_EPS) * g


def _dot(a, b):
    return jnp.dot(a, b, preferred_element_type=F32)


def _sigmoid(x):
    return 1.0 / (1.0 + jnp.exp(-x))


def _ffn_kernel(x_ref, g_ref, wgu_ref, wd_ref, fg_ref, o_ref, *, final_norm):
    x = x_ref[...]
    h = _rmsnorm(x, g_ref[...]).astype(BF16)
    fc = D_FF // FFN_CHUNKS
    acc = None
    for c in range(FFN_CHUNKS):
        gate = _dot(h, wgu_ref[:, c * fc:(c + 1) * fc])
        up = _dot(h, wgu_ref[:, D_FF + c * fc:D_FF + (c + 1) * fc])
        a = (gate * _sigmoid(gate) * up).astype(BF16)
        part = _dot(a, wd_ref[c * fc:(c + 1) * fc, :])
        acc = part if acc is None else acc + part
    y = x + 0.5 * acc
    if final_norm:
        y = _rmsnorm(y, fg_ref[...])
    o_ref[...] = y


def _ffn(x2d, g, w_gu, w_down, final_g, final_norm):
    n = x2d.shape[0]
    row = pl.BlockSpec((FFN_ROWS, D_MODEL), lambda i: (i, 0))
    return pl.pallas_call(
        functools.partial(_ffn_kernel, final_norm=final_norm),
        grid=(n // FFN_ROWS,),
        in_specs=[row, _resident((1, D_MODEL)), _resident((D_MODEL, 2 * D_FF)),
                  _resident((D_FF, D_MODEL)), _resident((1, D_MODEL))],
        out_specs=row,
        out_shape=jax.ShapeDtypeStruct((n, D_MODEL), F32),
        compiler_params=_params("arbitrary"),
        name="ffn",
    )(x2d, g.reshape(1, D_MODEL), w_gu.astype(BF16), w_down.astype(BF16),
      final_g.reshape(1, D_MODEL))


def _log_sigmoid(z):
    return jnp.minimum(z, 0.0) - jnp.log1p(jnp.exp(-jnp.abs(z)))


def _cumsum_lanes(v):
    n = v.shape[-1]
    lane = lax.broadcasted_iota(jnp.int32, v.shape, v.ndim - 1)
    s = 1
    while s < n:
        v = v + jnp.where(lane >= s, pltpu.roll(v, s, v.ndim - 1), 0.0)
        s *= 2
    return v


def _fox_proj_kernel(x_ref, g_ref, wqkv_ref, wft_ref, bf_ref, q_ref, k_ref, v_ref, f_ref,
                     carry_ref):
    @pl.when(pl.program_id(1) == 0)
    def _():
        carry_ref[...] = jnp.zeros_like(carry_ref)

    h = _rmsnorm(x_ref[0], g_ref[...]).astype(BF16)
    qkv = _dot(h, wqkv_ref[...])
    q_ref[0] = (qkv[:, :D_MODEL] * (FOX_HEAD_DIM ** -0.5)).astype(BF16)
    k_ref[0] = qkv[:, D_MODEL:2 * D_MODEL].astype(BF16)
    v_ref[0] = qkv[:, 2 * D_MODEL:].astype(BF16)
    f_logit = lax.dot_general(wft_ref[...], h, (((1,), (1,)), ((), ())),
                              preferred_element_type=F32)
    log_f = _log_sigmoid(f_logit + bf_ref[...])
    cum = _cumsum_lanes(log_f) + carry_ref[:, 0:1]
    f_ref[0] = cum
    carry_ref[...] = jnp.broadcast_to(cum[:, PROJ_ROWS - 1:PROJ_ROWS], carry_ref.shape)


def _fox_proj(x, g, w_in, b_f):
    b, t, _ = x.shape
    w_qkv = w_in[:, :3 * D_MODEL].astype(BF16)
    w_ft = w_in[:, 3 * D_MODEL:].T.astype(BF16)
    row = pl.BlockSpec((1, PROJ_ROWS, D_MODEL), lambda bi, ti: (bi, ti, 0))
    qkv_shape = jax.ShapeDtypeStruct((b, t, D_MODEL), BF16)
    return pl.pallas_call(
        _fox_proj_kernel,
        grid=(b, t // PROJ_ROWS),
        in_specs=[row, _resident((1, D_MODEL)), _resident((D_MODEL, 3 * D_MODEL)),
                  _resident((FOX_HEADS, D_MODEL)), _resident((FOX_HEADS, 1))],
        out_specs=[row, row, row,
                   pl.BlockSpec((1, FOX_HEADS, PROJ_ROWS), lambda bi, ti: (bi, 0, ti))],
        out_shape=[qkv_shape, qkv_shape, qkv_shape,
                   jax.ShapeDtypeStruct((b, FOX_HEADS, t), F32)],
        scratch_shapes=[pltpu.VMEM((FOX_HEADS, LANES), F32)],
        compiler_params=_params("arbitrary", "arbitrary"),
        name="fox_proj",
    )(x, g.reshape(1, D_MODEL), w_qkv, w_ft, b_f.reshape(FOX_HEADS, 1))


def _fox_attn_kernel(qi_ref, ki_ref, q_ref, k_ref, v_ref, fq_ref, fk_ref, o_ref,
                     m_ref, l_ref, acc_ref):
    step = pl.program_id(1)
    qi = qi_ref[step]
    ki = ki_ref[step]
    blk = ATTN_BLOCK
    dh = FOX_HEAD_DIM

    @pl.when(ki == 0)
    def _():
        m_ref[...] = jnp.full_like(m_ref, -jnp.inf)
        l_ref[...] = jnp.zeros_like(l_ref)
        acc_ref[...] = jnp.zeros_like(acc_ref)

    def block(masked):
        if masked:
            row = lax.broadcasted_iota(jnp.int32, (blk, blk), 0)
            col = lax.broadcasted_iota(jnp.int32, (blk, blk), 1)
            keep = col <= row
        for h in range(FOX_HEADS):
            sl = slice(h * dh, (h + 1) * dh)
            s = lax.dot_general(q_ref[0, :, sl], k_ref[0, :, sl], (((1,), (1,)), ((), ())),
                                preferred_element_type=F32)
            s = s + (fq_ref[0, :, h:h + 1] - fk_ref[0, h:h + 1, :])
            if masked:
                s = jnp.where(keep, s, -jnp.inf)
            m_prev = m_ref[h]
            m_new = jnp.maximum(m_prev, jnp.max(s, axis=1, keepdims=True))
            alpha = jnp.exp(m_prev - m_new)
            p = jnp.exp(s - m_new)
            l_ref[h] = alpha * l_ref[h] + jnp.sum(p, axis=1, keepdims=True)
            acc_ref[:, sl] = alpha * acc_ref[:, sl] + _dot(p.astype(BF16), v_ref[0, :, sl])
            m_ref[h] = m_new

    @pl.when(ki < qi)
    def _():
        block(False)

    @pl.when(ki == qi)
    def _():
        block(True)
        for h in range(FOX_HEADS):
            sl = slice(h * dh, (h + 1) * dh)
            o_ref[0, :, sl] = (acc_ref[:, sl] / l_ref[h]).astype(BF16)


def _fox_attn(q, k, v, f_row):
    b, t, _ = q.shape
    nb = t // ATTN_BLOCK
    f_col = jnp.transpose(f_row, (0, 2, 1))
    qi_tab = np.concatenate([np.full(i + 1, i) for i in range(nb)]).astype(np.int32)
    ki_tab = np.concatenate([np.arange(i + 1) for i in range(nb)]).astype(np.int32)
    q_spec = pl.BlockSpec((1, ATTN_BLOCK, D_MODEL), lambda bi, s, qi, ki: (bi, qi[s], 0))
    kv_spec = pl.BlockSpec((1, ATTN_BLOCK, D_MODEL), lambda bi, s, qi, ki: (bi, ki[s], 0))
    grid_spec = pltpu.PrefetchScalarGridSpec(
        num_scalar_prefetch=2,
        grid=(b, len(qi_tab)),
        in_specs=[q_spec, kv_spec, kv_spec,
                  pl.BlockSpec((1, ATTN_BLOCK, FOX_HEADS), lambda bi, s, qi, ki: (bi, qi[s], 0)),
                  pl.BlockSpec((1, FOX_HEADS, ATTN_BLOCK), lambda bi, s, qi, ki: (bi, 0, ki[s]))],
        out_specs=q_spec,
        scratch_shapes=[pltpu.VMEM((FOX_HEADS, ATTN_BLOCK, 1), F32),
                        pltpu.VMEM((FOX_HEADS, ATTN_BLOCK, 1), F32),
                        pltpu.VMEM((ATTN_BLOCK, D_MODEL), F32)],
    )
    return pl.pallas_call(
        _fox_attn_kernel,
        grid_spec=grid_spec,
        out_shape=jax.ShapeDtypeStruct((b, t, D_MODEL), BF16),
        compiler_params=_params("arbitrary", "arbitrary"),
        name="fox_attn",
    )(jnp.asarray(qi_tab), jnp.asarray(ki_tab), q, k, v, f_col, f_row)


def _out_proj_kernel(x_ref, o_ref, w_ref, y_ref):
    y_ref[...] = x_ref[...] + _dot(o_ref[...], w_ref[...])


def _out_proj(x2d, o2d, w):
    n = x2d.shape[0]
    k = o2d.shape[1]
    row = pl.BlockSpec((PROJ_ROWS, D_MODEL), lambda i: (i, 0))
    return pl.pallas_call(
        _out_proj_kernel,
        grid=(n // PROJ_ROWS,),
        in_specs=[row, pl.BlockSpec((PROJ_ROWS, k), lambda i: (i, 0)), _resident((k, D_MODEL))],
        out_specs=row,
        out_shape=jax.ShapeDtypeStruct((n, D_MODEL), F32),
        compiler_params=_params("arbitrary"),
        name="out_proj",
    )(x2d, o2d, w.astype(BF16))


def _gelu_tanh(x):
    c = np.float32(np.sqrt(2.0 / np.pi))
    return 0.5 * x * (1.0 + jnp.tanh(c * (x + 0.044715 * (x * x * x))))


def _softplus(z):
    return jnp.maximum(z, 0.0) + jnp.log1p(jnp.exp(-jnp.abs(z)))


def _linear_scan_rows(a, b):
    n = a.shape[0]
    row = lax.broadcasted_iota(jnp.int32, a.shape, 0)
    s = 1
    while s < n:
        keep = row >= s
        b = jnp.where(keep, a * pltpu.roll(b, s, 0) + b, b)
        a = jnp.where(keep, a * pltpu.roll(a, s, 0), a)
        s *= 2
    return a, b


def _lru_kernel(x_ref, g_ref, win_ref, cw_ref, cb_ref, wg_ref, ba_ref, bi_ref, lam_ref,
                wout_ref, y_ref, ext_ref, h_ref):
    rows = LRU_ROWS
    r = RNN_WIDTH

    @pl.when(pl.program_id(1) == 0)
    def _():
        ext_ref[0:SUBLANES, :] = jnp.zeros((SUBLANES, r), F32)
        h_ref[...] = jnp.zeros_like(h_ref)

    x = x_ref[0]
    h = _rmsnorm(x, g_ref[...]).astype(BF16)
    gr = _dot(h, win_ref[...])
    gate = _gelu_tanh(gr[:, :r])
    rec = gr[:, r:]
    ext_ref[SUBLANES:, :] = rec
    base = SUBLANES - (CONV_WIDTH - 1)
    xc = cb_ref[...] + ext_ref[base:base + rows, :] * cw_ref[0:1, :]
    for j in range(1, CONV_WIDTH):
        xc = xc + ext_ref[base + j:base + j + rows, :] * cw_ref[j:j + 1, :]
    ext_ref[0:SUBLANES, :] = rec[rows - SUBLANES:, :]
    xb = xc.astype(BF16)
    gw = LRU_GROUP_W
    ga, gi = [], []
    for c in range(LRU_GROUPS):
        both = _dot(xb[:, c * gw:(c + 1) * gw], wg_ref[c])
        ga.append(both[:, :gw])
        gi.append(both[:, gw:])
    rg = _sigmoid(jnp.concatenate(ga, axis=1) + ba_ref[...])
    ig = _sigmoid(jnp.concatenate(gi, axis=1) + bi_ref[...])
    log_a = (-RG_C * _softplus(-lam_ref[...])) * rg
    a = jnp.exp(log_a)
    th = jnp.tanh(log_a)
    bb = jnp.sqrt(-2.0 * th / (1.0 - th)) * (ig * xc)
    a_cum, hs = _linear_scan_rows(a, bb)
    hs = hs + a_cum * h_ref[0:1, :]
    h_ref[...] = jnp.broadcast_to(hs[rows - 1:rows, :], h_ref.shape)
    y = (hs * gate).astype(BF16)
    y_ref[0] = x + _dot(y, wout_ref[...])


def _block_diag_groups(w):
    per = RNN_BLOCKS // LRU_GROUPS
    out = jnp.zeros((LRU_GROUPS, LRU_GROUP_W, LRU_GROUP_W), w.dtype)
    for n in range(RNN_BLOCKS):
        g, j = divmod(n, per)
        out = out.at[g, j * RNN_BLOCK_W:(j + 1) * RNN_BLOCK_W,
                     j * RNN_BLOCK_W:(j + 1) * RNN_BLOCK_W].set(w[n])
    return out


def _lru(x, g, w_in, conv_w, conv_b, w_a, b_a, w_i, b_i, lam, w_out):
    b, t, _ = x.shape
    r = RNN_WIDTH
    w_gates = jnp.concatenate([_block_diag_groups(w_a), _block_diag_groups(w_i)],
                              axis=2).astype(BF16)
    row = pl.BlockSpec((1, LRU_ROWS, D_MODEL), lambda bi, ti: (bi, ti, 0))
    vec = _resident((1, r))
    return pl.pallas_call(
        _lru_kernel,
        grid=(b, t // LRU_ROWS),
        in_specs=[row, _resident((1, D_MODEL)), _resident((D_MODEL, 2 * r)),
                  _resident((CONV_WIDTH, r)), vec,
                  _resident((LRU_GROUPS, LRU_GROUP_W, 2 * LRU_GROUP_W)), vec, vec, vec,
                  _resident((r, D_MODEL))],
        out_specs=row,
        out_shape=jax.ShapeDtypeStruct((b, t, D_MODEL), F32),
        scratch_shapes=[pltpu.VMEM((LRU_ROWS + SUBLANES, r), F32),
                        pltpu.VMEM((SUBLANES, r), F32)],
        compiler_params=_params("arbitrary", "arbitrary"),
        name="rglru",
    )(x, g.reshape(1, D_MODEL), w_in.astype(BF16), conv_w, conv_b.reshape(1, r), w_gates,
      b_a.reshape(1, r), b_i.reshape(1, r), lam.reshape(1, r), w_out.astype(BF16))


def kernel(x, ffn1_norm, ffn1_w_gu, ffn1_w_down, mix_norm, ffn2_norm, ffn2_w_gu, ffn2_w_down,
           fox_w_in, fox_b_f, fox_w_out, lru_w_in, lru_conv_w, lru_conv_b, lru_w_a, lru_b_a,
           lru_w_i, lru_b_i, lru_lambda, lru_w_out, final_norm):
    b, t, d = x.shape
    n = b * t
    for i in range(DEPTH):
        x = _ffn(x.reshape(n, d), ffn1_norm[i], ffn1_w_gu[i], ffn1_w_down[i], final_norm,
                 False).reshape(b, t, d)
        j = i // N_MIXERS
        if i % N_MIXERS == 0:
            q, k, v, f_row = _fox_proj(x, mix_norm[i], fox_w_in[j], fox_b_f[j])
            o = _fox_attn(q, k, v, f_row)
            x = _out_proj(x.reshape(n, d), o.reshape(n, d), fox_w_out[j]).reshape(b, t, d)
        else:
            x = _lru(x, mix_norm[i], lru_w_in[j], lru_conv_w[j], lru_conv_b[j], lru_w_a[j],
                     lru_b_a[j], lru_w_i[j], lru_b_i[j], lru_lambda[j], lru_w_out[j])
        x = _ffn(x.reshape(n, d), ffn2_norm[i], ffn2_w_gu[i], ffn2_w_down[i], final_norm,
                 i == DEPTH - 1).reshape(b, t, d)
    return x
```

```python
import functools

import jax
import jax.numpy as jnp
import numpy as np
from jax import lax
from jax.experimental import pallas as pl
from jax.experimental.pallas import tpu as pltpu

D_MODEL = 1024
DEPTH = 2
N_MIXERS = 2
FOX_HEADS = 16
FOX_HEAD_DIM = D_MODEL // FOX_HEADS
RNN_WIDTH = 1280
RNN_BLOCKS = 16
RNN_BLOCK_W = RNN_WIDTH // RNN_BLOCKS
CONV_WIDTH = 4
RG_C = 8.0
D_FF = 2816
RMS_EPS = 1e-6
LOG2E = float(np.log2(np.e))

LANES = 128
SUBLANES = 8
BF16_SUBLANES = 16
VMEM_LIMIT_BYTES = 56 * 1024 * 1024

FFN_ROWS = 512
FFN_CHUNKS = 2
PROJ_ROWS = 512
ATTN_BLOCK = 512
LRU_ROWS = 256
LRU_GROUPS = 2
LRU_GROUP_W = RNN_WIDTH // LRU_GROUPS

HEAD_K = LANES
AUG_SLOTS = SUBLANES
F_PIECES = 3
V_ROWS = FOX_HEAD_DIM + BF16_SUBLANES

BF16 = jnp.bfloat16
F32 = jnp.float32


def _params(*sem):
    return pltpu.CompilerParams(dimension_semantics=sem, vmem_limit_bytes=VMEM_LIMIT_BYTES)


def _resident(shape):
    nd = len(shape)
    return pl.BlockSpec(shape, lambda *_: (0,) * nd, pipeline_mode=pl.Buffered(1))


def _rmsnorm(x, g):
    ms = jnp.mean(x * x, axis=-1, keepdims=True)
    return x * lax.rsqrt(ms + RMS_EPS) * g


def _dot(a, b):
    return jnp.dot(a, b, preferred_element_type=F32)


def _dot_nt(a, b):
    return lax.dot_general(a, b, (((1,), (1,)), ((), ())), preferred_element_type=F32)


def _sigmoid(x):
    return 1.0 / (1.0 + jnp.exp(-x))


def _ffn_kernel(x_ref, g_ref, wgu_ref, wd_ref, fg_ref, o_ref, *, final_norm):
    x = x_ref[...]
    h = _rmsnorm(x, g_ref[...]).astype(BF16)
    fc = D_FF // FFN_CHUNKS
    acc = None
    for c in range(FFN_CHUNKS):
        gate = _dot(h, wgu_ref[:, c * fc:(c + 1) * fc])
        up = _dot(h, wgu_ref[:, D_FF + c * fc:D_FF + (c + 1) * fc])
        a = (gate * _sigmoid(gate) * up).astype(BF16)
        part = _dot(a, wd_ref[c * fc:(c + 1) * fc, :])
        acc = part if acc is None else acc + part
    y = x + 0.5 * acc
    if final_norm:
        y = _rmsnorm(y, fg_ref[...])
    o_ref[...] = y


def _ffn(x2d, g, w_gu, w_down, final_g, final_norm):
    n = x2d.shape[0]
    row = pl.BlockSpec((FFN_ROWS, D_MODEL), lambda i: (i, 0))
    return pl.pallas_call(
        functools.partial(_ffn_kernel, final_norm=final_norm),
        grid=(n // FFN_ROWS,),
        in_specs=[row, _resident((1, D_MODEL)), _resident((D_MODEL, 2 * D_FF)),
                  _resident((D_FF, D_MODEL)), _resident((1, D_MODEL))],
        out_specs=row,
        out_shape=jax.ShapeDtypeStruct((n, D_MODEL), F32),
        compiler_params=_params("arbitrary"),
        name="ffn",
    )(x2d, g.reshape(1, D_MODEL), w_gu.astype(BF16), w_down.astype(BF16),
      final_g.reshape(1, D_MODEL))


def _log_sigmoid(z):
    return jnp.minimum(z, 0.0) - jnp.log1p(jnp.exp(-jnp.abs(z)))


def _cumsum_lanes(v):
    n = v.shape[-1]
    lane = lax.broadcasted_iota(jnp.int32, v.shape, v.ndim - 1)
    s = 1
    while s < n:
        v = v + jnp.where(lane >= s, pltpu.roll(v, s, v.ndim - 1), 0.0)
        s *= 2
    return v


def _split_bf16_pieces(f):
    p1 = f.astype(BF16).astype(F32)
    r1 = f - p1
    p2 = r1.astype(BF16).astype(F32)
    return p1, p2, r1 - p2


def _fox_proj_kernel(x_ref, g_ref, wqv_ref, wk_ref, place_ref, wft_ref, bf_ref,
                     ka_ref, qt_ref, vt_ref, carry_ref):
    rows = PROJ_ROWS
    dh = FOX_HEAD_DIM

    @pl.when(pl.program_id(1) == 0)
    def _():
        carry_ref[...] = jnp.zeros_like(carry_ref)

    h = _rmsnorm(x_ref[0], g_ref[...]).astype(BF16)
    log_f = _log_sigmoid(_dot_nt(wft_ref[...], h) + bf_ref[...])
    cum = _cumsum_lanes(log_f) + carry_ref[:, 0:1]
    carry_ref[...] = jnp.broadcast_to(cum[:, rows - 1:rows], carry_ref.shape)
    pieces = _split_bf16_pieces(cum * LOG2E)

    qv_t = _dot_nt(wqv_ref[...], h)
    sub_q = lax.broadcasted_iota(jnp.int32, (BF16_SUBLANES, rows), 0)
    sub_k = lax.broadcasted_iota(jnp.int32, (AUG_SLOTS, rows), 0)
    vt_tail = jnp.where(sub_q == 0, 1.0, 0.0).astype(BF16)
    q_scale = LOG2E * dh ** -0.5
    ka_aug = []
    for hh in range(FOX_HEADS):
        qt_ref[0, hh, 0:dh, :] = (qv_t[hh * dh:(hh + 1) * dh] * q_scale).astype(BF16)
        aug_q = jnp.where(sub_q < 2 * F_PIECES, 1.0, 0.0)
        aug_k = jnp.where(sub_k < F_PIECES, 1.0, 0.0)
        for j in range(F_PIECES):
            pj = pieces[j][hh:hh + 1]
            aug_q = jnp.where(sub_q == j, pj, aug_q)
            aug_k = jnp.where(sub_k == F_PIECES + j, -pj, aug_k)
        qt_ref[0, hh, dh:dh + BF16_SUBLANES, :] = aug_q.astype(BF16)
        qt_ref[0, hh, dh + BF16_SUBLANES:, :] = jnp.zeros(
            (HEAD_K - dh - BF16_SUBLANES, rows), BF16)
        vt_ref[0, hh, 0:dh, :] = qv_t[D_MODEL + hh * dh:D_MODEL + (hh + 1) * dh].astype(BF16)
        vt_ref[0, hh, dh:, :] = vt_tail
        ka_aug.append(aug_k)
    ka_tok = jnp.concatenate(ka_aug, axis=0).T.astype(BF16)
    k_wide = _dot(h, wk_ref[...]) + _dot(ka_tok, place_ref[...])
    for hh in range(FOX_HEADS):
        ka_ref[0, hh] = k_wide[:, hh * HEAD_K:(hh + 1) * HEAD_K].astype(BF16)


def _aug_placement():
    p = np.zeros((FOX_HEADS * AUG_SLOTS, FOX_HEADS * HEAD_K), np.float32)
    for hh in range(FOX_HEADS):
        for j in range(AUG_SLOTS):
            p[hh * AUG_SLOTS + j, hh * HEAD_K + FOX_HEAD_DIM + j] = 1.0
    return p


def _fox_proj(x, g, w_in, b_f):
    b, t, d = x.shape
    hk = FOX_HEADS * HEAD_K
    w_q, w_k, w_v, w_f = (w_in[:, :d], w_in[:, d:2 * d], w_in[:, 2 * d:3 * d], w_in[:, 3 * d:])
    w_qv_t = jnp.concatenate([w_q, w_v], axis=1).T.astype(BF16)
    w_k_wide = jnp.pad(w_k.reshape(d, FOX_HEADS, FOX_HEAD_DIM),
                       ((0, 0), (0, 0), (0, HEAD_K - FOX_HEAD_DIM))).reshape(d, hk).astype(BF16)
    place = jnp.asarray(_aug_placement(), BF16)
    row = pl.BlockSpec((1, PROJ_ROWS, d), lambda bi, ti: (bi, ti, 0))
    return pl.pallas_call(
        _fox_proj_kernel,
        grid=(b, t // PROJ_ROWS),
        in_specs=[row, _resident((1, d)), _resident((2 * d, d)), _resident((d, hk)),
                  _resident((FOX_HEADS * AUG_SLOTS, hk)), _resident((FOX_HEADS, d)),
                  _resident((FOX_HEADS, 1))],
        out_specs=[pl.BlockSpec((1, FOX_HEADS, PROJ_ROWS, HEAD_K), lambda bi, ti: (bi, 0, ti, 0)),
                   pl.BlockSpec((1, FOX_HEADS, HEAD_K, PROJ_ROWS), lambda bi, ti: (bi, 0, 0, ti)),
                   pl.BlockSpec((1, FOX_HEADS, V_ROWS, PROJ_ROWS), lambda bi, ti: (bi, 0, 0, ti))],
        out_shape=[jax.ShapeDtypeStruct((b, FOX_HEADS, t, HEAD_K), BF16),
                   jax.ShapeDtypeStruct((b, FOX_HEADS, HEAD_K, t), BF16),
                   jax.ShapeDtypeStruct((b, FOX_HEADS, V_ROWS, t), BF16)],
        scratch_shapes=[pltpu.VMEM((FOX_HEADS, LANES), F32)],
        compiler_params=_params("arbitrary", "arbitrary"),
        name="fox_proj",
    )(x, g.reshape(1, d), w_qv_t, w_k_wide, place, w_f.T.astype(BF16),
      b_f.reshape(FOX_HEADS, 1))


def _fox_attn_kernel(qi_ref, ki_ref, ka_ref, qt_ref, vt_ref, o_ref, m_ref, acc_ref):
    step = pl.program_id(1)
    qi = qi_ref[step]
    ki = ki_ref[step]
    blk = ATTN_BLOCK
    dh = FOX_HEAD_DIM

    @pl.when(ki == 0)
    def _():
        m_ref[...] = jnp.full_like(m_ref, -jnp.inf)
        acc_ref[...] = jnp.zeros_like(acc_ref)

    def block(masked):
        if masked:
            key = lax.broadcasted_iota(jnp.int32, (blk, blk), 0)
            qry = lax.broadcasted_iota(jnp.int32, (blk, blk), 1)
            keep = key <= qry
        for h in range(FOX_HEADS):
            s = _dot(ka_ref[0, h], qt_ref[0, h])
            if masked:
                s = jnp.where(keep, s, -jnp.inf)
            m_prev = m_ref[h]
            m_new = jnp.maximum(m_prev, jnp.max(s, axis=0, keepdims=True))
            alpha = jnp.exp2(m_prev - m_new)
            p = jnp.exp2(s - m_new).astype(BF16)
            acc_ref[h] = alpha * acc_ref[h] + _dot(vt_ref[0, h], p)
            m_ref[h] = m_new

    @pl.when(ki < qi)
    def _():
        block(False)

    @pl.when(ki == qi)
    def _():
        block(True)
        for hp in range(FOX_HEADS // 2):
            pair = []
            for h in (2 * hp, 2 * hp + 1):
                a = acc_ref[h]
                pair.append(a[0:dh] * (1.0 / a[dh:dh + 1]))
            o_ref[0, :, hp * LANES:(hp + 1) * LANES] = (
                jnp.concatenate(pair, axis=0).T.astype(BF16))


def _fox_attn(ka, qt, vt):
    b, _, t, _ = ka.shape
    nb = t // ATTN_BLOCK
    qi_tab = np.concatenate([np.full(i + 1, i) for i in range(nb)]).astype(np.int32)
    ki_tab = np.concatenate([np.arange(i + 1) for i in range(nb)]).astype(np.int32)
    o_spec = pl.BlockSpec((1, ATTN_BLOCK, D_MODEL), lambda bi, s, qi, ki: (bi, qi[s], 0))
    grid_spec = pltpu.PrefetchScalarGridSpec(
        num_scalar_prefetch=2,
        grid=(b, len(qi_tab)),
        in_specs=[pl.BlockSpec((1, FOX_HEADS, ATTN_BLOCK, HEAD_K),
                               lambda bi, s, qi, ki: (bi, 0, ki[s], 0)),
                  pl.BlockSpec((1, FOX_HEADS, HEAD_K, ATTN_BLOCK),
                               lambda bi, s, qi, ki: (bi, 0, 0, qi[s])),
                  pl.BlockSpec((1, FOX_HEADS, V_ROWS, ATTN_BLOCK),
                               lambda bi, s, qi, ki: (bi, 0, 0, ki[s]))],
        out_specs=o_spec,
        scratch_shapes=[pltpu.VMEM((FOX_HEADS, 1, ATTN_BLOCK), F32),
                        pltpu.VMEM((FOX_HEADS, V_ROWS, ATTN_BLOCK), F32)],
    )
    return pl.pallas_call(
        _fox_attn_kernel,
        grid_spec=grid_spec,
        out_shape=jax.ShapeDtypeStruct((b, t, D_MODEL), BF16),
        compiler_params=_params("arbitrary", "arbitrary"),
        name="fox_attn",
    )(jnp.asarray(qi_tab), jnp.asarray(ki_tab), ka, qt, vt)


def _out_proj_kernel(x_ref, o_ref, w_ref, y_ref):
    y_ref[...] = x_ref[...] + _dot(o_ref[...], w_ref[...])


def _out_proj(x2d, o2d, w):
    n = x2d.shape[0]
    k = o2d.shape[1]
    row = pl.BlockSpec((PROJ_ROWS, D_MODEL), lambda i: (i, 0))
    return pl.pallas_call(
        _out_proj_kernel,
        grid=(n // PROJ_ROWS,),
        in_specs=[row, pl.BlockSpec((PROJ_ROWS, k), lambda i: (i, 0)), _resident((k, D_MODEL))],
        out_specs=row,
        out_shape=jax.ShapeDtypeStruct((n, D_MODEL), F32),
        compiler_params=_params("arbitrary"),
        name="out_proj",
    )(x2d, o2d, w.astype(BF16))


def _gelu_tanh(x):
    c = np.float32(np.sqrt(2.0 / np.pi))
    return 0.5 * x * (1.0 + jnp.tanh(c * (x + 0.044715 * (x * x * x))))


def _softplus(z):
    return jnp.maximum(z, 0.0) + jnp.log1p(jnp.exp(-jnp.abs(z)))


def _linear_scan_rows(a, b):
    n = a.shape[0]
    row = lax.broadcasted_iota(jnp.int32, a.shape, 0)
    s = 1
    while s < n:
        keep = row >= s
        b = jnp.where(keep, a * pltpu.roll(b, s, 0) + b, b)
        a = jnp.where(keep, a * pltpu.roll(a, s, 0), a)
        s *= 2
    return a, b


def _lru_kernel(x_ref, g_ref, win_ref, cw_ref, cb_ref, wg_ref, ba_ref, bi_ref, lam_ref,
                wout_ref, y_ref, ext_ref, h_ref):
    rows = LRU_ROWS
    r = RNN_WIDTH

    @pl.when(pl.program_id(1) == 0)
    def _():
        ext_ref[0:SUBLANES, :] = jnp.zeros((SUBLANES, r), F32)
        h_ref[...] = jnp.zeros_like(h_ref)

    x = x_ref[0]
    h = _rmsnorm(x, g_ref[...]).astype(BF16)
    gr = _dot(h, win_ref[...])
    gate = _gelu_tanh(gr[:, :r])
    rec = gr[:, r:]
    ext_ref[SUBLANES:, :] = rec
    base = SUBLANES - (CONV_WIDTH - 1)
    xc = cb_ref[...] + ext_ref[base:base + rows, :] * cw_ref[0:1, :]
    for j in range(1, CONV_WIDTH):
        xc = xc + ext_ref[base + j:base + j + rows, :] * cw_ref[j:j + 1, :]
    ext_ref[0:SUBLANES, :] = rec[rows - SUBLANES:, :]
    xb = xc.astype(BF16)
    gw = LRU_GROUP_W
    ga, gi = [], []
    for c in range(LRU_GROUPS):
        both = _dot(xb[:, c * gw:(c + 1) * gw], wg_ref[c])
        ga.append(both[:, :gw])
        gi.append(both[:, gw:])
    rg = _sigmoid(jnp.concatenate(ga, axis=1) + ba_ref[...])
    ig = _sigmoid(jnp.concatenate(gi, axis=1) + bi_ref[...])
    log_a = (-RG_C * _softplus(-lam_ref[...])) * rg
    a = jnp.exp(log_a)
    th = jnp.tanh(log_a)
    bb = jnp.sqrt(-2.0 * th / (1.0 - th)) * (ig * xc)
    a_cum, hs = _linear_scan_rows(a, bb)
    hs = hs + a_cum * h_ref[0:1, :]
    h_ref[...] = jnp.broadcast_to(hs[rows - 1:rows, :], h_ref.shape)
    y = (hs * gate).astype(BF16)
    y_ref[0] = x + _dot(y, wout_ref[...])


def _block_diag_groups(w):
    per = RNN_BLOCKS // LRU_GROUPS
    out = jnp.zeros((LRU_GROUPS, LRU_GROUP_W, LRU_GROUP_W), w.dtype)
    for n in range(RNN_BLOCKS):
        g, j = divmod(n, per)
        out = out.at[g, j * RNN_BLOCK_W:(j + 1) * RNN_BLOCK_W,
                     j * RNN_BLOCK_W:(j + 1) * RNN_BLOCK_W].set(w[n])
    return out


def _lru(x, g, w_in, conv_w, conv_b, w_a, b_a, w_i, b_i, lam, w_out):
    b, t, _ = x.shape
    r = RNN_WIDTH
    w_gates = jnp.concatenate([_block_diag_groups(w_a), _block_diag_groups(w_i)],
                              axis=2).astype(BF16)
    row = pl.BlockSpec((1, LRU_ROWS, D_MODEL), lambda bi, ti: (bi, ti, 0))
    vec = _resident((1, r))
    return pl.pallas_call(
        _lru_kernel,
        grid=(b, t // LRU_ROWS),
        in_specs=[row, _resident((1, D_MODEL)), _resident((D_MODEL, 2 * r)),
                  _resident((CONV_WIDTH, r)), vec,
                  _resident((LRU_GROUPS, LRU_GROUP_W, 2 * LRU_GROUP_W)), vec, vec, vec,
                  _resident((r, D_MODEL))],
        out_specs=row,
        out_shape=jax.ShapeDtypeStruct((b, t, D_MODEL), F32),
        scratch_shapes=[pltpu.VMEM((LRU_ROWS + SUBLANES, r), F32),
                        pltpu.VMEM((SUBLANES, r), F32)],
        compiler_params=_params("arbitrary", "arbitrary"),
        name="rglru",
    )(x, g.reshape(1, D_MODEL), w_in.astype(BF16), conv_w, conv_b.reshape(1, r), w_gates,
      b_a.reshape(1, r), b_i.reshape(1, r), lam.reshape(1, r), w_out.astype(BF16))


def kernel(x, ffn1_norm, ffn1_w_gu, ffn1_w_down, mix_norm, ffn2_norm, ffn2_w_gu, ffn2_w_down,
           fox_w_in, fox_b_f, fox_w_out, lru_w_in, lru_conv_w, lru_conv_b, lru_w_a, lru_b_a,
           lru_w_i, lru_b_i, lru_lambda, lru_w_out, final_norm):
    b, t, d = x.shape
    n = b * t
    for i in range(DEPTH):
        x = _ffn(x.reshape(n, d), ffn1_norm[i], ffn1_w_gu[i], ffn1_w_down[i], final_norm,
                 False).reshape(b, t, d)
        j = i // N_MIXERS
        if i % N_MIXERS == 0:
            ka, qt, vt = _fox_proj(x, mix_norm[i], fox_w_in[j], fox_b_f[j])
            o = _fox_attn(ka, qt, vt)
            x = _out_proj(x.reshape(n, d), o.reshape(n, d), fox_w_out[j]).reshape(b, t, d)
        else:
            x = _lru(x, mix_norm[i], lru_w_in[j], lru_conv_w[j], lru_conv_b[j], lru_w_a[j],
                     lru_b_a[j], lru_w_i[j], lru_b_i[j], lru_lambda[j], lru_w_out[j])
        x = _ffn(x.reshape(n, d), ffn2_norm[i], ffn2_w_gu[i], ffn2_w_down[i], final_norm,
                 i == DEPTH - 1).reshape(b, t, d)
    return x
```

```python
import functools

import jax
import jax.numpy as jnp
import numpy as np
from jax import lax
from jax.experimental import pallas as pl
from jax.experimental.pallas import tpu as pltpu

D_MODEL = 1024
DEPTH = 2
N_MIXERS = 2
FOX_HEADS = 16
FOX_HEAD_DIM = D_MODEL // FOX_HEADS
RNN_WIDTH = 1280
RNN_BLOCKS = 16
RNN_BLOCK_W = RNN_WIDTH // RNN_BLOCKS
CONV_WIDTH = 4
RG_C = 8.0
D_FF = 2816
RMS_EPS = 1e-6
LOG2E = float(np.log2(np.e))

LANES = 128
SUBLANES = 8
BF16_SUBLANES = 16
VMEM_LIMIT_BYTES = 56 * 1024 * 1024

FFN_ROWS = 1024
FFN_SUB_ROWS = 512
MXU_TILE = 256
FFN_CHUNK_BOUNDS = (0, 6 * MXU_TILE, D_FF)
PROJ_ROWS = 512
ATTN_BLOCK = 512
ATTN_LOOKAHEAD = 2
LRU_STEP_ROWS = 256
LRU_ROWS = 256
LRU_GROUPS = 2
LRU_GROUP_W = RNN_WIDTH // LRU_GROUPS

HEAD_K = LANES
AUG_SLOTS = SUBLANES
F_PIECES = 3
V_ROWS = FOX_HEAD_DIM + BF16_SUBLANES

BF16 = jnp.bfloat16
F32 = jnp.float32


def _params(*sem):
    return pltpu.CompilerParams(dimension_semantics=sem, vmem_limit_bytes=VMEM_LIMIT_BYTES)


def _resident(shape):
    nd = len(shape)
    return pl.BlockSpec(shape, lambda *_: (0,) * nd, pipeline_mode=pl.Buffered(1))


def _rmsnorm(x, g):
    ms = jnp.mean(x * x, axis=-1, keepdims=True)
    return x * lax.rsqrt(ms + RMS_EPS) * g


def _dot(a, b):
    return jnp.dot(a, b, preferred_element_type=F32)


def _dot_nt(a, b):
    return lax.dot_general(a, b, (((1,), (1,)), ((), ())), preferred_element_type=F32)


def _sigmoid(x):
    return 1.0 / (1.0 + jnp.exp(-x))


def _ffn_kernel(x_ref, g_ref, wgu_ref, wd_ref, fg_ref, o_ref, *, final_norm):
    chunks = list(zip(FFN_CHUNK_BOUNDS[:-1], FFN_CHUNK_BOUNDS[1:]))
    n_sub = FFN_ROWS // FFN_SUB_ROWS

    def rows(i):
        return slice(i * FFN_SUB_ROWS, (i + 1) * FFN_SUB_ROWS)

    def normed(i):
        return _rmsnorm(x_ref[rows(i), :], g_ref[...]).astype(BF16)

    def finish(i, acc):
        y = x_ref[rows(i), :] + 0.5 * acc
        if final_norm:
            y = _rmsnorm(y, fg_ref[...])
        o_ref[rows(i), :] = y

    h = normed(0)
    done = None
    for i in range(n_sub):
        acc = None
        for c, (lo, hi) in enumerate(chunks):
            gate = _dot(h, wgu_ref[:, lo:hi])
            if c == 0:
                h_next = normed(i + 1) if i + 1 < n_sub else None
                if done is not None:
                    finish(*done)
            up = _dot(h, wgu_ref[:, D_FF + lo:D_FF + hi])
            a = (gate * _sigmoid(gate) * up).astype(BF16)
            part = _dot(a, wd_ref[lo:hi, :])
            acc = part if acc is None else acc + part
        done = (i, acc)
        h = h_next
    finish(*done)


def _ffn(x2d, g, w_gu, w_down, final_g, final_norm):
    n = x2d.shape[0]
    row = pl.BlockSpec((FFN_ROWS, D_MODEL), lambda i: (i, 0))
    return pl.pallas_call(
        functools.partial(_ffn_kernel, final_norm=final_norm),
        grid=(n // FFN_ROWS,),
        in_specs=[row, _resident((1, D_MODEL)), _resident((D_MODEL, 2 * D_FF)),
                  _resident((D_FF, D_MODEL)), _resident((1, D_MODEL))],
        out_specs=row,
        out_shape=jax.ShapeDtypeStruct((n, D_MODEL), F32),
        compiler_params=_params("arbitrary"),
        name="ffn",
    )(x2d, g.reshape(1, D_MODEL), w_gu.astype(BF16), w_down.astype(BF16),
      final_g.reshape(1, D_MODEL))


def _log_sigmoid(z):
    return jnp.minimum(z, 0.0) - jnp.log1p(jnp.exp(-jnp.abs(z)))


def _cumsum_lanes(v):
    n = v.shape[-1]
    lane = lax.broadcasted_iota(jnp.int32, v.shape, v.ndim - 1)
    s = 1
    while s < n:
        v = v + jnp.where(lane >= s, pltpu.roll(v, s, v.ndim - 1), 0.0)
        s *= 2
    return v


def _split_bf16_pieces(f):
    p1 = f.astype(BF16).astype(F32)
    r1 = f - p1
    p2 = r1.astype(BF16).astype(F32)
    return p1, p2, r1 - p2


def _fox_proj_kernel(x_ref, g_ref, wqv_ref, wk_ref, place_ref, wft_ref, bf_ref,
                     ka_ref, qt_ref, vt_ref, carry_ref):
    rows = PROJ_ROWS
    dh = FOX_HEAD_DIM

    @pl.when(pl.program_id(1) == 0)
    def _():
        carry_ref[...] = jnp.zeros_like(carry_ref)

    h = _rmsnorm(x_ref[0], g_ref[...]).astype(BF16)
    log_f = _log_sigmoid(_dot_nt(wft_ref[...], h) + bf_ref[...])
    cum = _cumsum_lanes(log_f) + carry_ref[:, 0:1]
    carry_ref[...] = jnp.broadcast_to(cum[:, rows - 1:rows], carry_ref.shape)
    pieces = _split_bf16_pieces(cum * LOG2E)

    qv_t = _dot_nt(wqv_ref[...], h)
    sub_q = lax.broadcasted_iota(jnp.int32, (BF16_SUBLANES, rows), 0)
    sub_k = lax.broadcasted_iota(jnp.int32, (AUG_SLOTS, rows), 0)
    vt_tail = jnp.where(sub_q == 0, 1.0, 0.0).astype(BF16)
    q_scale = LOG2E * dh ** -0.5
    ka_aug = []
    for hh in range(FOX_HEADS):
        qt_ref[0, hh, 0:dh, :] = (qv_t[hh * dh:(hh + 1) * dh] * q_scale).astype(BF16)
        aug_q = jnp.where(sub_q < 2 * F_PIECES, 1.0, 0.0)
        aug_k = jnp.where(sub_k < F_PIECES, 1.0, 0.0)
        for j in range(F_PIECES):
            pj = pieces[j][hh:hh + 1]
            aug_q = jnp.where(sub_q == j, pj, aug_q)
            aug_k = jnp.where(sub_k == F_PIECES + j, -pj, aug_k)
        qt_ref[0, hh, dh:dh + BF16_SUBLANES, :] = aug_q.astype(BF16)
        qt_ref[0, hh, dh + BF16_SUBLANES:, :] = jnp.zeros(
            (HEAD_K - dh - BF16_SUBLANES, rows), BF16)
        vt_ref[0, hh, 0:dh, :] = qv_t[D_MODEL + hh * dh:D_MODEL + (hh + 1) * dh].astype(BF16)
        vt_ref[0, hh, dh:, :] = vt_tail
        ka_aug.append(aug_k)
    ka_tok = jnp.concatenate(ka_aug, axis=0).T.astype(BF16)
    k_wide = _dot(h, wk_ref[...]) + _dot(ka_tok, place_ref[...])
    for hh in range(FOX_HEADS):
        ka_ref[0, hh] = k_wide[:, hh * HEAD_K:(hh + 1) * HEAD_K].astype(BF16)


def _aug_placement():
    p = np.zeros((FOX_HEADS * AUG_SLOTS, FOX_HEADS * HEAD_K), np.float32)
    for hh in range(FOX_HEADS):
        for j in range(AUG_SLOTS):
            p[hh * AUG_SLOTS + j, hh * HEAD_K + FOX_HEAD_DIM + j] = 1.0
    return p


def _fox_proj(x, g, w_in, b_f):
    b, t, d = x.shape
    hk = FOX_HEADS * HEAD_K
    w_q, w_k, w_v, w_f = (w_in[:, :d], w_in[:, d:2 * d], w_in[:, 2 * d:3 * d], w_in[:, 3 * d:])
    w_qv_t = jnp.concatenate([w_q, w_v], axis=1).T.astype(BF16)
    w_k_wide = jnp.pad(w_k.reshape(d, FOX_HEADS, FOX_HEAD_DIM),
                       ((0, 0), (0, 0), (0, HEAD_K - FOX_HEAD_DIM))).reshape(d, hk).astype(BF16)
    place = jnp.asarray(_aug_placement(), BF16)
    row = pl.BlockSpec((1, PROJ_ROWS, d), lambda bi, ti: (bi, ti, 0))
    return pl.pallas_call(
        _fox_proj_kernel,
        grid=(b, t // PROJ_ROWS),
        in_specs=[row, _resident((1, d)), _resident((2 * d, d)), _resident((d, hk)),
                  _resident((FOX_HEADS * AUG_SLOTS, hk)), _resident((FOX_HEADS, d)),
                  _resident((FOX_HEADS, 1))],
        out_specs=[pl.BlockSpec((1, FOX_HEADS, PROJ_ROWS, HEAD_K), lambda bi, ti: (bi, 0, ti, 0)),
                   pl.BlockSpec((1, FOX_HEADS, HEAD_K, PROJ_ROWS), lambda bi, ti: (bi, 0, 0, ti)),
                   pl.BlockSpec((1, FOX_HEADS, V_ROWS, PROJ_ROWS), lambda bi, ti: (bi, 0, 0, ti))],
        out_shape=[jax.ShapeDtypeStruct((b, FOX_HEADS, t, HEAD_K), BF16),
                   jax.ShapeDtypeStruct((b, FOX_HEADS, HEAD_K, t), BF16),
                   jax.ShapeDtypeStruct((b, FOX_HEADS, V_ROWS, t), BF16)],
        scratch_shapes=[pltpu.VMEM((FOX_HEADS, LANES), F32)],
        compiler_params=_params("arbitrary", "arbitrary"),
        name="fox_proj",
    )(x, g.reshape(1, d), w_qv_t, w_k_wide, place, w_f.T.astype(BF16),
      b_f.reshape(FOX_HEADS, 1))


def _fox_attn_kernel(qi_ref, ki_ref, ka_ref, qt_ref, vt_ref, o_ref, m_ref, acc_ref):
    step = pl.program_id(1)
    qi = qi_ref[step]
    ki = ki_ref[step]
    blk = ATTN_BLOCK
    dh = FOX_HEAD_DIM

    @pl.when(ki == 0)
    def _():
        m_ref[...] = jnp.full_like(m_ref, -jnp.inf)
        acc_ref[...] = jnp.zeros_like(acc_ref)

    def block(masked):
        if masked:
            key = lax.broadcasted_iota(jnp.int32, (blk, blk), 0)
            qry = lax.broadcasted_iota(jnp.int32, (blk, blk), 1)
            keep = key <= qry

        def scores(h):
            s = _dot(ka_ref[0, h], qt_ref[0, h])
            if masked:
                s = jnp.where(keep, s, -jnp.inf)
            m_prev = m_ref[h]
            m_new = jnp.maximum(m_prev, jnp.max(s, axis=0, keepdims=True))
            m_ref[h] = m_new
            return s, m_new, jnp.exp2(m_prev - m_new)

        ahead = [scores(h) for h in range(ATTN_LOOKAHEAD)]
        for h in range(FOX_HEADS):
            s, m_new, alpha = ahead.pop(0)
            if h + ATTN_LOOKAHEAD < FOX_HEADS:
                ahead.append(scores(h + ATTN_LOOKAHEAD))
            p = jnp.exp2(s - m_new).astype(BF16)
            acc_ref[h] = alpha * acc_ref[h] + _dot(vt_ref[0, h], p)

    @pl.when(ki < qi)
    def _():
        block(False)

    @pl.when(ki == qi)
    def _():
        block(True)
        for hp in range(FOX_HEADS // 2):
            pair = []
            for h in (2 * hp, 2 * hp + 1):
                a = acc_ref[h]
                pair.append(a[0:dh] * (1.0 / a[dh:dh + 1]))
            o_ref[0, :, hp * LANES:(hp + 1) * LANES] = (
                jnp.concatenate(pair, axis=0).T.astype(BF16))


def _fox_attn(ka, qt, vt):
    b, _, t, _ = ka.shape
    nb = t // ATTN_BLOCK
    qi_tab = np.concatenate([np.full(i + 1, i) for i in range(nb)]).astype(np.int32)
    ki_tab = np.concatenate([np.arange(i + 1) for i in range(nb)]).astype(np.int32)
    o_spec = pl.BlockSpec((1, ATTN_BLOCK, D_MODEL), lambda bi, s, qi, ki: (bi, qi[s], 0))
    grid_spec = pltpu.PrefetchScalarGridSpec(
        num_scalar_prefetch=2,
        grid=(b, len(qi_tab)),
        in_specs=[pl.BlockSpec((1, FOX_HEADS, ATTN_BLOCK, HEAD_K),
                               lambda bi, s, qi, ki: (bi, 0, ki[s], 0)),
                  pl.BlockSpec((1, FOX_HEADS, HEAD_K, ATTN_BLOCK),
                               lambda bi, s, qi, ki: (bi, 0, 0, qi[s])),
                  pl.BlockSpec((1, FOX_HEADS, V_ROWS, ATTN_BLOCK),
                               lambda bi, s, qi, ki: (bi, 0, 0, ki[s]))],
        out_specs=o_spec,
        scratch_shapes=[pltpu.VMEM((FOX_HEADS, 1, ATTN_BLOCK), F32),
                        pltpu.VMEM((FOX_HEADS, V_ROWS, ATTN_BLOCK), F32)],
    )
    return pl.pallas_call(
        _fox_attn_kernel,
        grid_spec=grid_spec,
        out_shape=jax.ShapeDtypeStruct((b, t, D_MODEL), BF16),
        compiler_params=_params("arbitrary", "arbitrary"),
        name="fox_attn",
    )(jnp.asarray(qi_tab), jnp.asarray(ki_tab), ka, qt, vt)


def _out_proj_kernel(x_ref, o_ref, w_ref, y_ref):
    y_ref[...] = x_ref[...] + _dot(o_ref[...], w_ref[...])


def _out_proj(x2d, o2d, w):
    n = x2d.shape[0]
    k = o2d.shape[1]
    row = pl.BlockSpec((PROJ_ROWS, D_MODEL), lambda i: (i, 0))
    return pl.pallas_call(
        _out_proj_kernel,
        grid=(n // PROJ_ROWS,),
        in_specs=[row, pl.BlockSpec((PROJ_ROWS, k), lambda i: (i, 0)), _resident((k, D_MODEL))],
        out_specs=row,
        out_shape=jax.ShapeDtypeStruct((n, D_MODEL), F32),
        compiler_params=_params("arbitrary"),
        name="out_proj",
    )(x2d, o2d, w.astype(BF16))


def _gelu_tanh(x):
    k1 = float(2.0 * np.sqrt(2.0 / np.pi))
    return x * _sigmoid(x * (k1 + (k1 * 0.044715) * (x * x)))


def _softplus(z):
    return jnp.maximum(z, 0.0) + jnp.log1p(jnp.exp(-jnp.abs(z)))


def _linear_scan_rows(a, b, h0):
    rows, n = a.shape
    groups = rows // SUBLANES
    a = a.reshape(groups, SUBLANES, n)
    b = b.reshape(groups, SUBLANES, n)
    sub = lax.broadcasted_iota(jnp.int32, a.shape, 1)
    s = 1
    while s < SUBLANES:
        keep = sub >= s
        b = jnp.where(keep, a * pltpu.roll(b, s, 1) + b, b)
        a = jnp.where(keep, a * pltpu.roll(a, s, 1), a)
        s *= 2
    out = []
    carry = h0
    for g in range(groups):
        hg = b[g] + a[g] * carry
        out.append(hg)
        carry = hg[SUBLANES - 1:SUBLANES, :]
    return jnp.concatenate(out, axis=0)


def _lru_kernel(x_ref, g_ref, win_ref, cw_ref, cb_ref, wg_ref, ba_ref, bi_ref, lam_ref,
                wout_ref, y_ref, ext_ref, h_ref):
    rows = LRU_ROWS
    r = RNN_WIDTH

    @pl.when(pl.program_id(1) == 0)
    def _():
        ext_ref[0:SUBLANES, :] = jnp.zeros((SUBLANES, r), F32)
        h_ref[...] = jnp.zeros_like(h_ref)

    def sub(i):
        return slice(i * rows, (i + 1) * rows)

    def in_proj(i):
        h = _rmsnorm(x_ref[0, sub(i), :], g_ref[...]).astype(BF16)
        return _dot(h, win_ref[...])

    def recurrence(gr, h_prev):
        gate = _gelu_tanh(gr[:, :r])
        rec = gr[:, r:]
        ext_ref[SUBLANES:, :] = rec
        base = SUBLANES - (CONV_WIDTH - 1)
        xc = cb_ref[...] + ext_ref[base:base + rows, :] * cw_ref[0:1, :]
        for j in range(1, CONV_WIDTH):
            xc = xc + ext_ref[base + j:base + j + rows, :] * cw_ref[j:j + 1, :]
        ext_ref[0:SUBLANES, :] = rec[rows - SUBLANES:, :]
        xb = xc.astype(BF16)
        gw = LRU_GROUP_W
        ga, gi = [], []
        for c in range(LRU_GROUPS):
            both = _dot(xb[:, c * gw:(c + 1) * gw], wg_ref[c])
            ga.append(both[:, :gw])
            gi.append(both[:, gw:])
        rg = _sigmoid(jnp.concatenate(ga, axis=1) + ba_ref[...])
        ig = _sigmoid(jnp.concatenate(gi, axis=1) + bi_ref[...])
        log_a = (-RG_C * _softplus(-lam_ref[...])) * rg
        a = jnp.exp(log_a)
        th = jnp.tanh(log_a)
        bb = jnp.sqrt(-2.0 * th / (1.0 - th)) * (ig * xc)
        hs = _linear_scan_rows(a, bb, h_prev)
        return (hs * gate).astype(BF16), hs[rows - 1:rows, :]

    n_sub = LRU_STEP_ROWS // rows
    carry = h_ref[0:1, :]
    gr = in_proj(0)
    for i in range(n_sub):
        gr_next = in_proj(i + 1) if i + 1 < n_sub else None
        y, carry = recurrence(gr, carry)
        y_ref[0, sub(i), :] = x_ref[0, sub(i), :] + _dot(y, wout_ref[...])
        gr = gr_next
    h_ref[...] = jnp.broadcast_to(carry, h_ref.shape)


def _block_diag_groups(w):
    per = RNN_BLOCKS // LRU_GROUPS
    out = jnp.zeros((LRU_GROUPS, LRU_GROUP_W, LRU_GROUP_W), w.dtype)
    for n in range(RNN_BLOCKS):
        g, j = divmod(n, per)
        out = out.at[g, j * RNN_BLOCK_W:(j + 1) * RNN_BLOCK_W,
                     j * RNN_BLOCK_W:(j + 1) * RNN_BLOCK_W].set(w[n])
    return out


def _lru(x, g, w_in, conv_w, conv_b, w_a, b_a, w_i, b_i, lam, w_out):
    b, t, _ = x.shape
    r = RNN_WIDTH
    w_gates = jnp.concatenate([_block_diag_groups(w_a), _block_diag_groups(w_i)],
                              axis=2).astype(BF16)
    row = pl.BlockSpec((1, LRU_STEP_ROWS, D_MODEL), lambda bi, ti: (bi, ti, 0))
    vec = _resident((1, r))
    return pl.pallas_call(
        _lru_kernel,
        grid=(b, t // LRU_STEP_ROWS),
        in_specs=[row, _resident((1, D_MODEL)), _resident((D_MODEL, 2 * r)),
                  _resident((CONV_WIDTH, r)), vec,
                  _resident((LRU_GROUPS, LRU_GROUP_W, 2 * LRU_GROUP_W)), vec, vec, vec,
                  _resident((r, D_MODEL))],
        out_specs=row,
        out_shape=jax.ShapeDtypeStruct((b, t, D_MODEL), F32),
        scratch_shapes=[pltpu.VMEM((LRU_ROWS + SUBLANES, r), F32),
                        pltpu.VMEM((SUBLANES, r), F32)],
        compiler_params=_params("arbitrary", "arbitrary"),
        name="rglru",
    )(x, g.reshape(1, D_MODEL), w_in.astype(BF16), conv_w, conv_b.reshape(1, r), w_gates,
      b_a.reshape(1, r), b_i.reshape(1, r), lam.reshape(1, r), w_out.astype(BF16))


def kernel(x, ffn1_norm, ffn1_w_gu, ffn1_w_down, mix_norm, ffn2_norm, ffn2_w_gu, ffn2_w_down,
           fox_w_in, fox_b_f, fox_w_out, lru_w_in, lru_conv_w, lru_conv_b, lru_w_a, lru_b_a,
           lru_w_i, lru_b_i, lru_lambda, lru_w_out, final_norm):
    b, t, d = x.shape
    n = b * t
    for i in range(DEPTH):
        x = _ffn(x.reshape(n, d), ffn1_norm[i], ffn1_w_gu[i], ffn1_w_down[i], final_norm,
                 False).reshape(b, t, d)
        j = i // N_MIXERS
        if i % N_MIXERS == 0:
            ka, qt, vt = _fox_proj(x, mix_norm[i], fox_w_in[j], fox_b_f[j])
            o = _fox_attn(ka, qt, vt)
            x = _out_proj(x.reshape(n, d), o.reshape(n, d), fox_w_out[j]).reshape(b, t, d)
        else:
            x = _lru(x, mix_norm[i], lru_w_in[j], lru_conv_w[j], lru_conv_b[j], lru_w_a[j],
                     lru_b_a[j], lru_w_i[j], lru_b_i[j], lru_lambda[j], lru_w_out[j])
        x = _ffn(x.reshape(n, d), ffn2_norm[i], ffn2_w_gu[i], ffn2_w_down[i], final_norm,
                 i == DEPTH - 1).reshape(b, t, d)
    return x
```

```python
import functools

import jax
import jax.numpy as jnp
import numpy as np
from jax import lax
from jax.experimental import pallas as pl
from jax.experimental.pallas import tpu as pltpu

D_MODEL = 1024
DEPTH = 2
N_MIXERS = 2
FOX_HEADS = 16
FOX_HEAD_DIM = D_MODEL // FOX_HEADS
RNN_WIDTH = 1280
RNN_BLOCKS = 16
RNN_BLOCK_W = RNN_WIDTH // RNN_BLOCKS
CONV_WIDTH = 4
RG_C = 8.0
D_FF = 2816
RMS_EPS = 1e-6
LOG2E = float(np.log2(np.e))

LANES = 128
SUBLANES = 8
BF16_SUBLANES = 16
VMEM_LIMIT_BYTES = 56 * 1024 * 1024

FFN_ROWS = 1024
FFN_SUB_ROWS = 512
MXU_TILE = 256
FFN_CHUNK_BOUNDS = (0, 6 * MXU_TILE, D_FF)
PROJ_ROWS = 512
ATTN_BLOCK = 512
ATTN_LOOKAHEAD = 2
LRU_ROWS = 256
LRU_GROUPS = 2
LRU_GROUP_W = RNN_WIDTH // LRU_GROUPS

HEAD_K = LANES
AUG_SLOTS = SUBLANES
F_PIECES = 3
V_ROWS = FOX_HEAD_DIM + BF16_SUBLANES

BF16 = jnp.bfloat16
F32 = jnp.float32


def _params(*sem):
    return pltpu.CompilerParams(dimension_semantics=sem, vmem_limit_bytes=VMEM_LIMIT_BYTES)


def _resident(shape):
    nd = len(shape)
    return pl.BlockSpec(shape, lambda *_: (0,) * nd, pipeline_mode=pl.Buffered(1))


def _rmsnorm(x, g):
    ms = jnp.mean(x * x, axis=-1, keepdims=True)
    return x * lax.rsqrt(ms + RMS_EPS) * g


def _dot(a, b):
    return jnp.dot(a, b, preferred_element_type=F32)


def _dot_nt(a, b):
    return lax.dot_general(a, b, (((1,), (1,)), ((), ())), preferred_element_type=F32)


def _sigmoid(x):
    return 1.0 / (1.0 + jnp.exp2(x * (-LOG2E)))


def _ffn_kernel(x_ref, g_ref, wgu_ref, wd_ref, fg_ref, o_ref, *, final_norm):
    chunks = list(zip(FFN_CHUNK_BOUNDS[:-1], FFN_CHUNK_BOUNDS[1:]))
    n_sub = FFN_ROWS // FFN_SUB_ROWS

    def rows(i):
        return slice(i * FFN_SUB_ROWS, (i + 1) * FFN_SUB_ROWS)

    def normed(i):
        return _rmsnorm(x_ref[rows(i), :], g_ref[...]).astype(BF16)

    def finish(i, acc):
        y = x_ref[rows(i), :] + 0.5 * acc
        if final_norm:
            y = _rmsnorm(y, fg_ref[...])
        o_ref[rows(i), :] = y

    h = normed(0)
    done = None
    for i in range(n_sub):
        acc = None
        for c, (lo, hi) in enumerate(chunks):
            gate = _dot(h, wgu_ref[:, lo:hi])
            if c == 0:
                h_next = normed(i + 1) if i + 1 < n_sub else None
                if done is not None:
                    finish(*done)
            up = _dot(h, wgu_ref[:, D_FF + lo:D_FF + hi])
            a = (gate * _sigmoid(gate) * up).astype(BF16)
            part = _dot(a, wd_ref[lo:hi, :])
            acc = part if acc is None else acc + part
        done = (i, acc)
        h = h_next
    finish(*done)


def _ffn(x2d, g, w_gu, w_down, final_g, final_norm):
    n = x2d.shape[0]
    row = pl.BlockSpec((FFN_ROWS, D_MODEL), lambda i: (i, 0))
    return pl.pallas_call(
        functools.partial(_ffn_kernel, final_norm=final_norm),
        grid=(n // FFN_ROWS,),
        in_specs=[row, _resident((1, D_MODEL)), _resident((D_MODEL, 2 * D_FF)),
                  _resident((D_FF, D_MODEL)), _resident((1, D_MODEL))],
        out_specs=row,
        out_shape=jax.ShapeDtypeStruct((n, D_MODEL), F32),
        compiler_params=_params("arbitrary"),
        name="ffn",
    )(x2d, g.reshape(1, D_MODEL), w_gu.astype(BF16), w_down.astype(BF16),
      final_g.reshape(1, D_MODEL))


def _log_sigmoid(z):
    return jnp.minimum(z, 0.0) - jnp.log1p(jnp.exp(-jnp.abs(z)))


def _cumsum_lanes(v):
    n = v.shape[-1]
    lane = lax.broadcasted_iota(jnp.int32, v.shape, v.ndim - 1)
    s = 1
    while s < n:
        v = v + jnp.where(lane >= s, pltpu.roll(v, s, v.ndim - 1), 0.0)
        s *= 2
    return v


def _split_bf16_pieces(f):
    p1 = f.astype(BF16).astype(F32)
    r1 = f - p1
    p2 = r1.astype(BF16).astype(F32)
    return p1, p2, r1 - p2


def _fox_proj_kernel(x_ref, g_ref, wqv_ref, wk_ref, place_ref, wft_ref, bf_ref,
                     ka_ref, qt_ref, vt_ref, carry_ref):
    rows = PROJ_ROWS
    dh = FOX_HEAD_DIM

    @pl.when(pl.program_id(1) == 0)
    def _():
        carry_ref[...] = jnp.zeros_like(carry_ref)

    h = _rmsnorm(x_ref[0], g_ref[...]).astype(BF16)
    log_f = _log_sigmoid(_dot_nt(wft_ref[...], h) + bf_ref[...])
    cum = _cumsum_lanes(log_f) + carry_ref[:, 0:1]
    carry_ref[...] = jnp.broadcast_to(cum[:, rows - 1:rows], carry_ref.shape)
    pieces = _split_bf16_pieces(cum * LOG2E)

    qv_t = _dot_nt(wqv_ref[...], h)
    sub_q = lax.broadcasted_iota(jnp.int32, (BF16_SUBLANES, rows), 0)
    sub_k = lax.broadcasted_iota(jnp.int32, (AUG_SLOTS, rows), 0)
    vt_tail = jnp.where(sub_q == 0, 1.0, 0.0).astype(BF16)
    q_scale = LOG2E * dh ** -0.5
    ka_aug = []
    for hh in range(FOX_HEADS):
        qt_ref[0, hh, 0:dh, :] = (qv_t[hh * dh:(hh + 1) * dh] * q_scale).astype(BF16)
        aug_q = jnp.where(sub_q < 2 * F_PIECES, 1.0, 0.0)
        aug_k = jnp.where(sub_k < F_PIECES, 1.0, 0.0)
        for j in range(F_PIECES):
            pj = pieces[j][hh:hh + 1]
            aug_q = jnp.where(sub_q == j, pj, aug_q)
            aug_k = jnp.where(sub_k == F_PIECES + j, -pj, aug_k)
        qt_ref[0, hh, dh:dh + BF16_SUBLANES, :] = aug_q.astype(BF16)
        qt_ref[0, hh, dh + BF16_SUBLANES:, :] = jnp.zeros(
            (HEAD_K - dh - BF16_SUBLANES, rows), BF16)
        vt_ref[0, hh, 0:dh, :] = qv_t[D_MODEL + hh * dh:D_MODEL + (hh + 1) * dh].astype(BF16)
        vt_ref[0, hh, dh:, :] = vt_tail
        ka_aug.append(aug_k)
    ka_tok = jnp.concatenate(ka_aug, axis=0).T.astype(BF16)
    k_wide = _dot(h, wk_ref[...]) + _dot(ka_tok, place_ref[...])
    for hh in range(FOX_HEADS):
        ka_ref[0, hh] = k_wide[:, hh * HEAD_K:(hh + 1) * HEAD_K].astype(BF16)


def _aug_placement():
    p = np.zeros((FOX_HEADS * AUG_SLOTS, FOX_HEADS * HEAD_K), np.float32)
    for hh in range(FOX_HEADS):
        for j in range(AUG_SLOTS):
            p[hh * AUG_SLOTS + j, hh * HEAD_K + FOX_HEAD_DIM + j] = 1.0
    return p


def _fox_block_spec(tile, index_map):
    return pl.BlockSpec((None, 1, FOX_HEADS) + tile, lambda *a: index_map(*a) + (0, 0, 0))


def _fox_proj(x, g, w_in, b_f):
    b, t, d = x.shape
    assert PROJ_ROWS == ATTN_BLOCK
    nb = t // PROJ_ROWS
    hk = FOX_HEADS * HEAD_K
    w_q, w_k, w_v, w_f = (w_in[:, :d], w_in[:, d:2 * d], w_in[:, 2 * d:3 * d], w_in[:, 3 * d:])
    w_qv_t = jnp.concatenate([w_q, w_v], axis=1).T.astype(BF16)
    w_k_wide = jnp.pad(w_k.reshape(d, FOX_HEADS, FOX_HEAD_DIM),
                       ((0, 0), (0, 0), (0, HEAD_K - FOX_HEAD_DIM))).reshape(d, hk).astype(BF16)
    place = jnp.asarray(_aug_placement(), BF16)
    row = pl.BlockSpec((1, PROJ_ROWS, d), lambda bi, ti: (bi, ti, 0))
    return pl.pallas_call(
        _fox_proj_kernel,
        grid=(b, nb),
        in_specs=[row, _resident((1, d)), _resident((2 * d, d)), _resident((d, hk)),
                  _resident((FOX_HEADS * AUG_SLOTS, hk)), _resident((FOX_HEADS, d)),
                  _resident((FOX_HEADS, 1))],
        out_specs=[_fox_block_spec((PROJ_ROWS, HEAD_K), lambda bi, ti: (bi, ti)),
                   _fox_block_spec((HEAD_K, PROJ_ROWS), lambda bi, ti: (bi, ti)),
                   _fox_block_spec((V_ROWS, PROJ_ROWS), lambda bi, ti: (bi, ti))],
        out_shape=[jax.ShapeDtypeStruct((b, nb, FOX_HEADS, PROJ_ROWS, HEAD_K), BF16),
                   jax.ShapeDtypeStruct((b, nb, FOX_HEADS, HEAD_K, PROJ_ROWS), BF16),
                   jax.ShapeDtypeStruct((b, nb, FOX_HEADS, V_ROWS, PROJ_ROWS), BF16)],
        scratch_shapes=[pltpu.VMEM((FOX_HEADS, LANES), F32)],
        compiler_params=_params("arbitrary", "arbitrary"),
        name="fox_proj",
    )(x, g.reshape(1, d), w_qv_t, w_k_wide, place, w_f.T.astype(BF16),
      b_f.reshape(FOX_HEADS, 1))


def _fox_attn_kernel(qi_ref, ki_ref, ka_ref, qt_ref, vt_ref, o_ref, m_ref, acc_ref):
    step = pl.program_id(1)
    qi = qi_ref[step]
    ki = ki_ref[step]
    blk = ATTN_BLOCK
    dh = FOX_HEAD_DIM

    @pl.when(ki == 0)
    def _():
        m_ref[...] = jnp.full_like(m_ref, -jnp.inf)
        acc_ref[...] = jnp.zeros_like(acc_ref)

    def block(masked):
        if masked:
            key = lax.broadcasted_iota(jnp.int32, (blk, blk), 0)
            qry = lax.broadcasted_iota(jnp.int32, (blk, blk), 1)
            keep = key <= qry

        def scores(h):
            s = _dot(ka_ref[0, h], qt_ref[0, h])
            if masked:
                s = jnp.where(keep, s, -jnp.inf)
            m_prev = m_ref[h]
            m_new = jnp.maximum(m_prev, jnp.max(s, axis=0, keepdims=True))
            m_ref[h] = m_new
            return s, m_new, jnp.exp2(m_prev - m_new)

        ahead = [scores(h) for h in range(ATTN_LOOKAHEAD)]
        for h in range(FOX_HEADS):
            s, m_new, alpha = ahead.pop(0)
            if h + ATTN_LOOKAHEAD < FOX_HEADS:
                ahead.append(scores(h + ATTN_LOOKAHEAD))
            p = jnp.exp2(s - m_new).astype(BF16)
            acc_ref[h] = alpha * acc_ref[h] + _dot(vt_ref[0, h], p)

    @pl.when(ki < qi)
    def _():
        block(False)

    @pl.when(ki == qi)
    def _():
        block(True)
        for hp in range(FOX_HEADS // 2):
            pair = []
            for h in (2 * hp, 2 * hp + 1):
                a = acc_ref[h]
                pair.append(a[0:dh] * (1.0 / a[dh:dh + 1]))
            o_ref[0, :, hp * LANES:(hp + 1) * LANES] = (
                jnp.concatenate(pair, axis=0).T.astype(BF16))


def _fox_attn(ka, qt, vt):
    b, nb = ka.shape[:2]
    t = nb * ATTN_BLOCK
    qi_tab = np.concatenate([np.full(i + 1, i) for i in range(nb)]).astype(np.int32)
    ki_tab = np.concatenate([np.arange(i + 1) for i in range(nb)]).astype(np.int32)
    o_spec = pl.BlockSpec((1, ATTN_BLOCK, D_MODEL), lambda bi, s, qi, ki: (bi, qi[s], 0))
    grid_spec = pltpu.PrefetchScalarGridSpec(
        num_scalar_prefetch=2,
        grid=(b, len(qi_tab)),
        in_specs=[_fox_block_spec((ATTN_BLOCK, HEAD_K), lambda bi, s, qi, ki: (bi, ki[s])),
                  _fox_block_spec((HEAD_K, ATTN_BLOCK), lambda bi, s, qi, ki: (bi, qi[s])),
                  _fox_block_spec((V_ROWS, ATTN_BLOCK), lambda bi, s, qi, ki: (bi, ki[s]))],
        out_specs=o_spec,
        scratch_shapes=[pltpu.VMEM((FOX_HEADS, 1, ATTN_BLOCK), F32),
                        pltpu.VMEM((FOX_HEADS, V_ROWS, ATTN_BLOCK), F32)],
    )
    return pl.pallas_call(
        _fox_attn_kernel,
        grid_spec=grid_spec,
        out_shape=jax.ShapeDtypeStruct((b, t, D_MODEL), BF16),
        compiler_params=_params("arbitrary", "arbitrary"),
        name="fox_attn",
    )(jnp.asarray(qi_tab), jnp.asarray(ki_tab), ka, qt, vt)


def _out_proj_kernel(x_ref, o_ref, w_ref, y_ref):
    y_ref[...] = x_ref[...] + _dot(o_ref[...], w_ref[...])


def _out_proj(x2d, o2d, w):
    n = x2d.shape[0]
    k = o2d.shape[1]
    row = pl.BlockSpec((PROJ_ROWS, D_MODEL), lambda i: (i, 0))
    return pl.pallas_call(
        _out_proj_kernel,
        grid=(n // PROJ_ROWS,),
        in_specs=[row, pl.BlockSpec((PROJ_ROWS, k), lambda i: (i, 0)), _resident((k, D_MODEL))],
        out_specs=row,
        out_shape=jax.ShapeDtypeStruct((n, D_MODEL), F32),
        compiler_params=_params("arbitrary"),
        name="out_proj",
    )(x2d, o2d, w.astype(BF16))


def _gelu_tanh(x):
    k1 = float(2.0 * np.sqrt(2.0 / np.pi))
    return x * _sigmoid(x * (k1 + (k1 * 0.044715) * (x * x)))


def _softplus(z):
    return jnp.maximum(z, 0.0) + jnp.log1p(jnp.exp(-jnp.abs(z)))


def _linear_scan_rows(a, b, h0):
    rows, n = a.shape
    groups = rows // SUBLANES
    a = a.reshape(groups, SUBLANES, n)
    b = b.reshape(groups, SUBLANES, n)
    sub = lax.broadcasted_iota(jnp.int32, a.shape, 1)
    s = 1
    while s < SUBLANES:
        keep = sub >= s
        b = jnp.where(keep, a * pltpu.roll(b, s, 1) + b, b)
        a = jnp.where(keep, a * pltpu.roll(a, s, 1), a)
        s *= 2
    out = []
    carry = h0
    for g in range(groups):
        hg = b[g] + a[g] * carry
        out.append(hg)
        carry = hg[SUBLANES - 1:SUBLANES, :]
    return jnp.concatenate(out, axis=0)


def _lru_kernel(x_ref, g_ref, win_ref, cw_ref, cb_ref, wg_ref, ba_ref, bi_ref, lam_ref,
                wout_ref, y_ref, ext_ref, h_ref):
    rows = LRU_ROWS
    r = RNN_WIDTH

    @pl.when(pl.program_id(1) == 0)
    def _():
        ext_ref[...] = jnp.zeros_like(ext_ref)
        h_ref[...] = jnp.zeros_like(h_ref)

    gw = LRU_GROUP_W
    x = x_ref[0]
    h = _rmsnorm(x, g_ref[...]).astype(BF16)

    def in_proj(c):
        return _dot(h, win_ref[:, c * 2 * gw:(c + 1) * 2 * gw])

    def recurrence(c, gr):
        lanes = slice(c * gw, (c + 1) * gw)
        gate = _gelu_tanh(gr[:, :gw])
        rec = gr[:, gw:]
        ext = jnp.concatenate([ext_ref[:, lanes], rec], axis=0)
        ext_ref[:, lanes] = rec[rows - SUBLANES:, :]
        xc = cb_ref[:, lanes]
        for j in range(CONV_WIDTH):
            lag = CONV_WIDTH - 1 - j
            past = pltpu.roll(ext, lag, 0)[SUBLANES:, :] if lag else rec
            xc = xc + past * cw_ref[j:j + 1, lanes]
        both = _dot(xc.astype(BF16), wg_ref[c])
        rg = _sigmoid(both[:, :gw] + ba_ref[:, lanes])
        ig = _sigmoid(both[:, gw:] + bi_ref[:, lanes])
        log_a = (-RG_C * _softplus(-lam_ref[:, lanes])) * rg
        a = jnp.exp(log_a)
        th = jnp.tanh(log_a)
        w = -2.0 * th / (1.0 - th)
        bb = jnp.where(w > 0.0, w * lax.rsqrt(w), 0.0) * (ig * xc)
        hs = _linear_scan_rows(a, bb, h_ref[0:1, lanes])
        h_ref[:, lanes] = jnp.broadcast_to(hs[rows - 1:rows, :], (SUBLANES, gw))
        return (hs * gate).astype(BF16)

    grs = [in_proj(c) for c in range(LRU_GROUPS)]
    y = jnp.concatenate([recurrence(c, grs[c]) for c in range(LRU_GROUPS)], axis=1)
    y_ref[0] = x + _dot(y, wout_ref[...])


def _block_diag_groups(w):
    per = RNN_BLOCKS // LRU_GROUPS
    out = jnp.zeros((LRU_GROUPS, LRU_GROUP_W, LRU_GROUP_W), w.dtype)
    for n in range(RNN_BLOCKS):
        g, j = divmod(n, per)
        out = out.at[g, j * RNN_BLOCK_W:(j + 1) * RNN_BLOCK_W,
                     j * RNN_BLOCK_W:(j + 1) * RNN_BLOCK_W].set(w[n])
    return out


def _lru(x, g, w_in, conv_w, conv_b, w_a, b_a, w_i, b_i, lam, w_out):
    b, t, _ = x.shape
    r = RNN_WIDTH
    w_gates = jnp.concatenate([_block_diag_groups(w_a), _block_diag_groups(w_i)],
                              axis=2).astype(BF16)
    w_in = w_in.reshape(D_MODEL, 2, LRU_GROUPS, LRU_GROUP_W).transpose(0, 2, 1, 3).reshape(
        D_MODEL, 2 * r)
    row = pl.BlockSpec((1, LRU_ROWS, D_MODEL), lambda bi, ti: (bi, ti, 0))
    vec = _resident((1, r))
    return pl.pallas_call(
        _lru_kernel,
        grid=(b, t // LRU_ROWS),
        in_specs=[row, _resident((1, D_MODEL)), _resident((D_MODEL, 2 * r)),
                  _resident((CONV_WIDTH, r)), vec,
                  _resident((LRU_GROUPS, LRU_GROUP_W, 2 * LRU_GROUP_W)), vec, vec, vec,
                  _resident((r, D_MODEL))],
        out_specs=row,
        out_shape=jax.ShapeDtypeStruct((b, t, D_MODEL), F32),
        scratch_shapes=[pltpu.VMEM((SUBLANES, r), F32),
                        pltpu.VMEM((SUBLANES, r), F32)],
        compiler_params=_params("arbitrary", "arbitrary"),
        name="rglru",
    )(x, g.reshape(1, D_MODEL), w_in.astype(BF16), conv_w, conv_b.reshape(1, r), w_gates,
      b_a.reshape(1, r), b_i.reshape(1, r), lam.reshape(1, r), w_out.astype(BF16))


def kernel(x, ffn1_norm, ffn1_w_gu, ffn1_w_down, mix_norm, ffn2_norm, ffn2_w_gu, ffn2_w_down,
           fox_w_in, fox_b_f, fox_w_out, lru_w_in, lru_conv_w, lru_conv_b, lru_w_a, lru_b_a,
           lru_w_i, lru_b_i, lru_lambda, lru_w_out, final_norm):
    b, t, d = x.shape
    n = b * t
    for i in range(DEPTH):
        x = _ffn(x.reshape(n, d), ffn1_norm[i], ffn1_w_gu[i], ffn1_w_down[i], final_norm,
                 False).reshape(b, t, d)
        j = i // N_MIXERS
        if i % N_MIXERS == 0:
            ka, qt, vt = _fox_proj(x, mix_norm[i], fox_w_in[j], fox_b_f[j])
            o = _fox_attn(ka, qt, vt)
            x = _out_proj(x.reshape(n, d), o.reshape(n, d), fox_w_out[j]).reshape(b, t, d)
        else:
            x = _lru(x, mix_norm[i], lru_w_in[j], lru_conv_w[j], lru_conv_b[j], lru_w_a[j],
                     lru_b_a[j], lru_w_i[j], lru_b_i[j], lru_lambda[j], lru_w_out[j])
        x = _ffn(x.reshape(n, d), ffn2_norm[i], ffn2_w_gu[i], ffn2_w_down[i], final_norm,
                 i == DEPTH - 1).reshape(b, t, d)
    return x
```

```python
import functools

import jax
import jax.numpy as jnp
import numpy as np
from jax import lax
from jax.experimental import pallas as pl
from jax.experimental.pallas import tpu as pltpu

D_MODEL = 1024
DEPTH = 2
N_MIXERS = 2
FOX_HEADS = 16
FOX_HEAD_DIM = D_MODEL // FOX_HEADS
RNN_WIDTH = 1280
RNN_BLOCKS = 16
RNN_BLOCK_W = RNN_WIDTH // RNN_BLOCKS
CONV_WIDTH = 4
RG_C = 8.0
D_FF = 2816
RMS_EPS = 1e-6
LOG2E = float(np.log2(np.e))

LANES = 128
SUBLANES = 8
BF16_SUBLANES = 16
VMEM_LIMIT_BYTES = 56 * 1024 * 1024

FFN_ROWS = 1024
FFN_SUB_ROWS = 512
FFN_STAGE_STEPS = 16
MXU_TILE = 256
FFN_CHUNK_BOUNDS = (0, 6 * MXU_TILE, D_FF)
PROJ_ROWS = 512
ATTN_BLOCK = 512
ATTN_LOOKAHEAD = 2
LRU_ROWS = 256
LRU_GROUPS = 2
LRU_GROUP_W = RNN_WIDTH // LRU_GROUPS

HEAD_K = LANES
AUG_SLOTS = SUBLANES
F_PIECES = 3
V_ROWS = FOX_HEAD_DIM + BF16_SUBLANES

BF16 = jnp.bfloat16
F32 = jnp.float32


def _params(*sem):
    return pltpu.CompilerParams(dimension_semantics=sem, vmem_limit_bytes=VMEM_LIMIT_BYTES)


def _resident(shape):
    nd = len(shape)
    return pl.BlockSpec(shape, lambda *_: (0,) * nd, pipeline_mode=pl.Buffered(1))


def _rmsnorm(x, g):
    ms = jnp.mean(x * x, axis=-1, keepdims=True)
    return x * lax.rsqrt(ms + RMS_EPS) * g


def _dot(a, b):
    return jnp.dot(a, b, preferred_element_type=F32)


def _dot_nt(a, b):
    return lax.dot_general(a, b, (((1,), (1,)), ((), ())), preferred_element_type=F32)


def _sigmoid(x):
    return 1.0 / (1.0 + jnp.exp2(x * (-LOG2E)))


def _ffn_kernel(*refs, final_norm, mixer_proj):
    if mixer_proj:
        x_ref, m_ref, mw_ref, g_ref, wgu32_ref, wd32_ref, fg_ref, o_ref, wgu_ref, wd_ref = refs
    else:
        x_ref, g_ref, wgu32_ref, wd32_ref, fg_ref, o_ref, wgu_ref, wd_ref = refs
        m_ref = mw_ref = None
    step = pl.program_id(0)

    @pl.when(step < FFN_STAGE_STEPS)
    def _():
        gu_rows, d_rows = wgu32_ref.shape[0], wd32_ref.shape[0]
        wgu_ref[pl.ds(pl.multiple_of(step * gu_rows, gu_rows), gu_rows), :] = (
            wgu32_ref[...].astype(BF16))
        wd_ref[pl.ds(pl.multiple_of(step * d_rows, d_rows), d_rows), :] = (
            wd32_ref[...].astype(BF16))

    @pl.when(step >= FFN_STAGE_STEPS)
    def _():
        _ffn_tile(x_ref, m_ref, mw_ref, g_ref, wgu_ref, wd_ref, fg_ref, o_ref, final_norm)


def _ffn_tile(x_ref, m_ref, mw_ref, g_ref, wgu_ref, wd_ref, fg_ref, o_ref, final_norm):
    chunks = list(zip(FFN_CHUNK_BOUNDS[:-1], FFN_CHUNK_BOUNDS[1:]))
    n_sub = FFN_ROWS // FFN_SUB_ROWS

    def rows(i):
        return slice(i * FFN_SUB_ROWS, (i + 1) * FFN_SUB_ROWS)

    def normed(i):
        x = x_ref[rows(i), :]
        if m_ref is not None:
            x = x + _dot(m_ref[rows(i), :], mw_ref[...])
            o_ref[rows(i), :] = x
        return _rmsnorm(x, g_ref[...]).astype(BF16)

    def finish(i, acc):
        base = x_ref if m_ref is None else o_ref
        y = base[rows(i), :] + 0.5 * acc
        if final_norm:
            y = _rmsnorm(y, fg_ref[...])
        o_ref[rows(i), :] = y

    h = normed(0)
    done = None
    for i in range(n_sub):
        acc = None
        for c, (lo, hi) in enumerate(chunks):
            gate = _dot(h, wgu_ref[:, lo:hi])
            if c == 0:
                h_next = normed(i + 1) if i + 1 < n_sub else None
                if done is not None:
                    finish(*done)
            up = _dot(h, wgu_ref[:, D_FF + lo:D_FF + hi])
            a = (gate * _sigmoid(gate) * up).astype(BF16)
            part = _dot(a, wd_ref[lo:hi, :])
            acc = part if acc is None else acc + part
        done = (i, acc)
        h = h_next
    finish(*done)


def _ffn(x2d, g, w_gu_all, w_down_all, layer, final_g, final_norm, mixer=None):
    n = x2d.shape[0]
    stage = FFN_STAGE_STEPS
    last = stage - 1

    def row_map(i):
        return (jnp.maximum(i - stage, 0), 0)

    row = pl.BlockSpec((FFN_ROWS, D_MODEL), row_map)
    operands, specs = [x2d], [row]
    if mixer is not None:
        m2d, w_m = mixer
        operands += [m2d, w_m.astype(BF16)]
        specs += [pl.BlockSpec((FFN_ROWS, m2d.shape[1]), row_map), _resident(w_m.shape)]
    operands += [g.reshape(1, D_MODEL), w_gu_all, w_down_all, final_g.reshape(1, D_MODEL)]
    specs += [_resident((1, D_MODEL)),
              pl.BlockSpec((None, D_MODEL // stage, 2 * D_FF),
                           lambda i: (layer, jnp.minimum(i, last), 0)),
              pl.BlockSpec((None, D_FF // stage, D_MODEL),
                           lambda i: (layer, jnp.minimum(i, last), 0)),
              _resident((1, D_MODEL))]
    return pl.pallas_call(
        functools.partial(_ffn_kernel, final_norm=final_norm, mixer_proj=mixer is not None),
        grid=(stage + n // FFN_ROWS,),
        in_specs=specs,
        out_specs=row,
        out_shape=jax.ShapeDtypeStruct((n, D_MODEL), F32),
        scratch_shapes=[pltpu.VMEM((D_MODEL, 2 * D_FF), BF16),
                        pltpu.VMEM((D_FF, D_MODEL), BF16)],
        compiler_params=_params("arbitrary"),
        name="ffn",
    )(*operands)


def _log_sigmoid(z):
    return jnp.minimum(z, 0.0) - jnp.log1p(jnp.exp(-jnp.abs(z)))


def _cumsum_lanes(v):
    n = v.shape[-1]
    lane = lax.broadcasted_iota(jnp.int32, v.shape, v.ndim - 1)
    s = 1
    while s < n:
        v = v + jnp.where(lane >= s, pltpu.roll(v, s, v.ndim - 1), 0.0)
        s *= 2
    return v


def _split_bf16_pieces(f):
    p1 = f.astype(BF16).astype(F32)
    r1 = f - p1
    p2 = r1.astype(BF16).astype(F32)
    return p1, p2, r1 - p2


def _fox_proj_kernel(x_ref, g_ref, wqv_ref, wk_ref, place_ref, wft_ref, bf_ref,
                     ka_ref, qt_ref, vt_ref, carry_ref):
    rows = PROJ_ROWS
    dh = FOX_HEAD_DIM

    @pl.when(pl.program_id(1) == 0)
    def _():
        carry_ref[...] = jnp.zeros_like(carry_ref)

    h = _rmsnorm(x_ref[0], g_ref[...]).astype(BF16)
    log_f = _log_sigmoid(_dot_nt(wft_ref[...], h) + bf_ref[...])
    cum = _cumsum_lanes(log_f) + carry_ref[:, 0:1]
    carry_ref[...] = jnp.broadcast_to(cum[:, rows - 1:rows], carry_ref.shape)
    pieces = _split_bf16_pieces(cum * LOG2E)

    qv_t = _dot_nt(wqv_ref[...], h)
    sub_q = lax.broadcasted_iota(jnp.int32, (BF16_SUBLANES, rows), 0)
    sub_k = lax.broadcasted_iota(jnp.int32, (AUG_SLOTS, rows), 0)
    vt_tail = jnp.where(sub_q == 0, 1.0, 0.0).astype(BF16)
    q_scale = LOG2E * dh ** -0.5
    ka_aug = []
    for hh in range(FOX_HEADS):
        qt_ref[0, hh, 0:dh, :] = (qv_t[hh * dh:(hh + 1) * dh] * q_scale).astype(BF16)
        aug_q = jnp.where(sub_q < 2 * F_PIECES, 1.0, 0.0)
        aug_k = jnp.where(sub_k < F_PIECES, 1.0, 0.0)
        for j in range(F_PIECES):
            pj = pieces[j][hh:hh + 1]
            aug_q = jnp.where(sub_q == j, pj, aug_q)
            aug_k = jnp.where(sub_k == F_PIECES + j, -pj, aug_k)
        qt_ref[0, hh, dh:dh + BF16_SUBLANES, :] = aug_q.astype(BF16)
        qt_ref[0, hh, dh + BF16_SUBLANES:, :] = jnp.zeros(
            (HEAD_K - dh - BF16_SUBLANES, rows), BF16)
        vt_ref[0, hh, 0:dh, :] = qv_t[D_MODEL + hh * dh:D_MODEL + (hh + 1) * dh].astype(BF16)
        vt_ref[0, hh, dh:, :] = vt_tail
        ka_aug.append(aug_k)
    ka_tok = jnp.concatenate(ka_aug, axis=0).T.astype(BF16)
    k_wide = _dot(h, wk_ref[...]) + _dot(ka_tok, place_ref[...])
    for hh in range(FOX_HEADS):
        ka_ref[0, hh] = k_wide[:, hh * HEAD_K:(hh + 1) * HEAD_K].astype(BF16)


def _aug_placement():
    p = np.zeros((FOX_HEADS * AUG_SLOTS, FOX_HEADS * HEAD_K), np.float32)
    for hh in range(FOX_HEADS):
        for j in range(AUG_SLOTS):
            p[hh * AUG_SLOTS + j, hh * HEAD_K + FOX_HEAD_DIM + j] = 1.0
    return p


def _fox_block_spec(tile, index_map):
    return pl.BlockSpec((None, 1, FOX_HEADS) + tile, lambda *a: index_map(*a) + (0, 0, 0))


def _fox_proj(x, g, w_in, b_f):
    b, t, d = x.shape
    assert PROJ_ROWS == ATTN_BLOCK
    nb = t // PROJ_ROWS
    hk = FOX_HEADS * HEAD_K
    w_q, w_k, w_v, w_f = (w_in[:, :d], w_in[:, d:2 * d], w_in[:, 2 * d:3 * d], w_in[:, 3 * d:])
    w_qv_t = jnp.concatenate([w_q, w_v], axis=1).T.astype(BF16)
    w_k_wide = jnp.pad(w_k.reshape(d, FOX_HEADS, FOX_HEAD_DIM),
                       ((0, 0), (0, 0), (0, HEAD_K - FOX_HEAD_DIM))).reshape(d, hk).astype(BF16)
    place = jnp.asarray(_aug_placement(), BF16)
    row = pl.BlockSpec((1, PROJ_ROWS, d), lambda bi, ti: (bi, ti, 0))
    return pl.pallas_call(
        _fox_proj_kernel,
        grid=(b, nb),
        in_specs=[row, _resident((1, d)), _resident((2 * d, d)), _resident((d, hk)),
                  _resident((FOX_HEADS * AUG_SLOTS, hk)), _resident((FOX_HEADS, d)),
                  _resident((FOX_HEADS, 1))],
        out_specs=[_fox_block_spec((PROJ_ROWS, HEAD_K), lambda bi, ti: (bi, ti)),
                   _fox_block_spec((HEAD_K, PROJ_ROWS), lambda bi, ti: (bi, ti)),
                   _fox_block_spec((V_ROWS, PROJ_ROWS), lambda bi, ti: (bi, ti))],
        out_shape=[jax.ShapeDtypeStruct((b, nb, FOX_HEADS, PROJ_ROWS, HEAD_K), BF16),
                   jax.ShapeDtypeStruct((b, nb, FOX_HEADS, HEAD_K, PROJ_ROWS), BF16),
                   jax.ShapeDtypeStruct((b, nb, FOX_HEADS, V_ROWS, PROJ_ROWS), BF16)],
        scratch_shapes=[pltpu.VMEM((FOX_HEADS, LANES), F32)],
        compiler_params=_params("arbitrary", "arbitrary"),
        name="fox_proj",
    )(x, g.reshape(1, d), w_qv_t, w_k_wide, place, w_f.T.astype(BF16),
      b_f.reshape(FOX_HEADS, 1))


def _fox_attn_kernel(qi_ref, ki_ref, ka_ref, qt_ref, vt_ref, o_ref, m_ref, acc_ref):
    step = pl.program_id(1)
    qi = qi_ref[step]
    ki = ki_ref[step]
    blk = ATTN_BLOCK
    dh = FOX_HEAD_DIM

    @pl.when(ki == 0)
    def _():
        m_ref[...] = jnp.full_like(m_ref, -jnp.inf)
        acc_ref[...] = jnp.zeros_like(acc_ref)

    def block(masked):
        if masked:
            key = lax.broadcasted_iota(jnp.int32, (blk, blk), 0)
            qry = lax.broadcasted_iota(jnp.int32, (blk, blk), 1)
            keep = key <= qry

        def scores(h):
            s = _dot(ka_ref[0, h], qt_ref[0, h])
            if masked:
                s = jnp.where(keep, s, -jnp.inf)
            m_prev = m_ref[h]
            m_new = jnp.maximum(m_prev, jnp.max(s, axis=0, keepdims=True))
            m_ref[h] = m_new
            return s, m_new, jnp.exp2(m_prev - m_new)

        ahead = [scores(h) for h in range(ATTN_LOOKAHEAD)]
        for h in range(FOX_HEADS):
            s, m_new, alpha = ahead.pop(0)
            if h + ATTN_LOOKAHEAD < FOX_HEADS:
                ahead.append(scores(h + ATTN_LOOKAHEAD))
            p = jnp.exp2(s - m_new).astype(BF16)
            acc_ref[h] = alpha * acc_ref[h] + _dot(vt_ref[0, h], p)

    @pl.when(ki < qi)
    def _():
        block(False)

    @pl.when(ki == qi)
    def _():
        block(True)
        for hp in range(FOX_HEADS // 2):
            pair = []
            for h in (2 * hp, 2 * hp + 1):
                a = acc_ref[h]
                pair.append(a[0:dh] * (1.0 / a[dh:dh + 1]))
            o_ref[0, :, hp * LANES:(hp + 1) * LANES] = (
                jnp.concatenate(pair, axis=0).T.astype(BF16))


def _fox_attn(ka, qt, vt):
    b, nb = ka.shape[:2]
    t = nb * ATTN_BLOCK
    qi_tab = np.concatenate([np.full(i + 1, i) for i in range(nb)]).astype(np.int32)
    ki_tab = np.concatenate([np.arange(i + 1) for i in range(nb)]).astype(np.int32)
    o_spec = pl.BlockSpec((1, ATTN_BLOCK, D_MODEL), lambda bi, s, qi, ki: (bi, qi[s], 0))
    grid_spec = pltpu.PrefetchScalarGridSpec(
        num_scalar_prefetch=2,
        grid=(b, len(qi_tab)),
        in_specs=[_fox_block_spec((ATTN_BLOCK, HEAD_K), lambda bi, s, qi, ki: (bi, ki[s])),
                  _fox_block_spec((HEAD_K, ATTN_BLOCK), lambda bi, s, qi, ki: (bi, qi[s])),
                  _fox_block_spec((V_ROWS, ATTN_BLOCK), lambda bi, s, qi, ki: (bi, ki[s]))],
        out_specs=o_spec,
        scratch_shapes=[pltpu.VMEM((FOX_HEADS, 1, ATTN_BLOCK), F32),
                        pltpu.VMEM((FOX_HEADS, V_ROWS, ATTN_BLOCK), F32)],
    )
    return pl.pallas_call(
        _fox_attn_kernel,
        grid_spec=grid_spec,
        out_shape=jax.ShapeDtypeStruct((b, t, D_MODEL), BF16),
        compiler_params=_params("arbitrary", "arbitrary"),
        name="fox_attn",
    )(jnp.asarray(qi_tab), jnp.asarray(ki_tab), ka, qt, vt)


def _gelu_tanh(x):
    k1 = float(2.0 * np.sqrt(2.0 / np.pi))
    return x * _sigmoid(x * (k1 + (k1 * 0.044715) * (x * x)))


def _softplus(z):
    return jnp.maximum(z, 0.0) + jnp.log1p(jnp.exp(-jnp.abs(z)))


def _linear_scan_rows(a, b, h0):
    rows, n = a.shape
    groups = rows // SUBLANES
    a = a.reshape(groups, SUBLANES, n)
    b = b.reshape(groups, SUBLANES, n)
    sub = lax.broadcasted_iota(jnp.int32, a.shape, 1)
    s = 1
    while s < SUBLANES:
        keep = sub >= s
        b = jnp.where(keep, a * pltpu.roll(b, s, 1) + b, b)
        a = jnp.where(keep, a * pltpu.roll(a, s, 1), a)
        s *= 2
    out = []
    carry = h0
    for g in range(groups):
        hg = b[g] + a[g] * carry
        out.append(hg)
        carry = hg[SUBLANES - 1:SUBLANES, :]
    return jnp.concatenate(out, axis=0)


def _lru_kernel(x_ref, g_ref, win_ref, cw_ref, cb_ref, wg_ref, ba_ref, bi_ref, lam_ref,
                wout_ref, y_ref, ext_ref, h_ref):
    rows = LRU_ROWS
    r = RNN_WIDTH

    @pl.when(pl.program_id(1) == 0)
    def _():
        ext_ref[...] = jnp.zeros_like(ext_ref)
        h_ref[...] = jnp.zeros_like(h_ref)

    gw = LRU_GROUP_W
    x = x_ref[0]
    h = _rmsnorm(x, g_ref[...]).astype(BF16)

    def in_proj(c):
        return _dot(h, win_ref[:, c * 2 * gw:(c + 1) * 2 * gw])

    def recurrence(c, gr):
        lanes = slice(c * gw, (c + 1) * gw)
        gate = _gelu_tanh(gr[:, :gw])
        rec = gr[:, gw:]
        ext = jnp.concatenate([ext_ref[:, lanes], rec], axis=0)
        ext_ref[:, lanes] = rec[rows - SUBLANES:, :]
        xc = cb_ref[:, lanes]
        for j in range(CONV_WIDTH):
            lag = CONV_WIDTH - 1 - j
            past = pltpu.roll(ext, lag, 0)[SUBLANES:, :] if lag else rec
            xc = xc + past * cw_ref[j:j + 1, lanes]
        both = _dot(xc.astype(BF16), wg_ref[c])
        rg = _sigmoid(both[:, :gw] + ba_ref[:, lanes])
        ig = _sigmoid(both[:, gw:] + bi_ref[:, lanes])
        log_a = (-RG_C * _softplus(-lam_ref[:, lanes])) * rg
        a = jnp.exp(log_a)
        th = jnp.tanh(log_a)
        w = -2.0 * th / (1.0 - th)
        bb = jnp.where(w > 0.0, w * lax.rsqrt(w), 0.0) * (ig * xc)
        hs = _linear_scan_rows(a, bb, h_ref[0:1, lanes])
        h_ref[:, lanes] = jnp.broadcast_to(hs[rows - 1:rows, :], (SUBLANES, gw))
        return (hs * gate).astype(BF16)

    grs = [in_proj(c) for c in range(LRU_GROUPS)]
    y = jnp.concatenate([recurrence(c, grs[c]) for c in range(LRU_GROUPS)], axis=1)
    y_ref[0] = x + _dot(y, wout_ref[...])


def _block_diag_groups(w):
    per = RNN_BLOCKS // LRU_GROUPS
    w = w.reshape(LRU_GROUPS, per, RNN_BLOCK_W, 1, RNN_BLOCK_W)
    on_diag = jnp.eye(per, dtype=bool).reshape(1, per, 1, per, 1)
    return jnp.where(on_diag, w, 0.0).reshape(LRU_GROUPS, LRU_GROUP_W, LRU_GROUP_W)


def _lru(x, g, w_in, conv_w, conv_b, w_a, b_a, w_i, b_i, lam, w_out):
    b, t, _ = x.shape
    r = RNN_WIDTH
    w_gates = jnp.concatenate([_block_diag_groups(w_a), _block_diag_groups(w_i)],
                              axis=2).astype(BF16)
    w_in = w_in.astype(BF16).reshape(D_MODEL, 2, LRU_GROUPS, LRU_GROUP_W).transpose(
        0, 2, 1, 3).reshape(D_MODEL, 2 * r)
    row = pl.BlockSpec((1, LRU_ROWS, D_MODEL), lambda bi, ti: (bi, ti, 0))
    vec = _resident((1, r))
    return pl.pallas_call(
        _lru_kernel,
        grid=(b, t // LRU_ROWS),
        in_specs=[row, _resident((1, D_MODEL)), _resident((D_MODEL, 2 * r)),
                  _resident((CONV_WIDTH, r)), vec,
                  _resident((LRU_GROUPS, LRU_GROUP_W, 2 * LRU_GROUP_W)), vec, vec, vec,
                  _resident((r, D_MODEL))],
        out_specs=row,
        out_shape=jax.ShapeDtypeStruct((b, t, D_MODEL), F32),
        scratch_shapes=[pltpu.VMEM((SUBLANES, r), F32),
                        pltpu.VMEM((SUBLANES, r), F32)],
        compiler_params=_params("arbitrary", "arbitrary"),
        name="rglru",
    )(x, g.reshape(1, D_MODEL), w_in.astype(BF16), conv_w, conv_b.reshape(1, r), w_gates,
      b_a.reshape(1, r), b_i.reshape(1, r), lam.reshape(1, r), w_out.astype(BF16))


def kernel(x, ffn1_norm, ffn1_w_gu, ffn1_w_down, mix_norm, ffn2_norm, ffn2_w_gu, ffn2_w_down,
           fox_w_in, fox_b_f, fox_w_out, lru_w_in, lru_conv_w, lru_conv_b, lru_w_a, lru_b_a,
           lru_w_i, lru_b_i, lru_lambda, lru_w_out, final_norm):
    b, t, d = x.shape
    n = b * t
    for i in range(DEPTH):
        x = _ffn(x.reshape(n, d), ffn1_norm[i], ffn1_w_gu, ffn1_w_down, i, final_norm,
                 False).reshape(b, t, d)
        j = i // N_MIXERS
        mixer = None
        if i % N_MIXERS == 0:
            ka, qt, vt = _fox_proj(x, mix_norm[i], fox_w_in[j], fox_b_f[j])
            mixer = (_fox_attn(ka, qt, vt).reshape(n, d), fox_w_out[j])
        else:
            x = _lru(x, mix_norm[i], lru_w_in[j], lru_conv_w[j], lru_conv_b[j], lru_w_a[j],
                     lru_b_a[j], lru_w_i[j], lru_b_i[j], lru_lambda[j], lru_w_out[j])
        x = _ffn(x.reshape(n, d), ffn2_norm[i], ffn2_w_gu, ffn2_w_down, i, final_norm,
                 i == DEPTH - 1, mixer).reshape(b, t, d)
    return x
```

```python
import functools

import jax
import jax.numpy as jnp
import numpy as np
from jax import lax
from jax.experimental import pallas as pl
from jax.experimental.pallas import tpu as pltpu

D_MODEL = 1024
DEPTH = 2
N_MIXERS = 2
FOX_HEADS = 16
FOX_HEAD_DIM = D_MODEL // FOX_HEADS
RNN_WIDTH = 1280
RNN_BLOCKS = 16
RNN_BLOCK_W = RNN_WIDTH // RNN_BLOCKS
CONV_WIDTH = 4
RG_C = 8.0
D_FF = 2816
RMS_EPS = 1e-6
LOG2E = float(np.log2(np.e))

LANES = 128
SUBLANES = 8
BF16_SUBLANES = 16
VMEM_LIMIT_BYTES = 56 * 1024 * 1024

FFN_ROWS = 1024
FFN_SUB_ROWS = 512
FFN_STAGE_STEPS = 16
MXU_TILE = 256
FFN_CHUNK_BOUNDS = (0, 6 * MXU_TILE, D_FF)
PROJ_ROWS = 512
ATTN_BLOCK = 512
ATTN_KEY_BLOCKS = 2
ATTN_LOOKAHEAD = 2
LRU_ROWS = 256
LRU_GROUPS = 2
LRU_GROUP_W = RNN_WIDTH // LRU_GROUPS

HEAD_K = LANES
AUG_SLOTS = SUBLANES
F_PIECES = 3
V_ROWS = FOX_HEAD_DIM + BF16_SUBLANES

BF16 = jnp.bfloat16
F32 = jnp.float32


def _params(*sem):
    return pltpu.CompilerParams(dimension_semantics=sem, vmem_limit_bytes=VMEM_LIMIT_BYTES)


def _resident(shape):
    nd = len(shape)
    return pl.BlockSpec(shape, lambda *_: (0,) * nd, pipeline_mode=pl.Buffered(1))


def _rmsnorm(x, g):
    ms = jnp.mean(x * x, axis=-1, keepdims=True)
    return x * lax.rsqrt(ms + RMS_EPS) * g


def _dot(a, b):
    return jnp.dot(a, b, preferred_element_type=F32)


def _dot_nt(a, b):
    return lax.dot_general(a, b, (((1,), (1,)), ((), ())), preferred_element_type=F32)


def _sigmoid(x):
    return 1.0 / (1.0 + jnp.exp2(x * (-LOG2E)))


def _ffn_kernel(*refs, final_norm, mixer_proj):
    if mixer_proj:
        x_ref, m_ref, mw_ref, g_ref, wgu32_ref, wd32_ref, fg_ref, o_ref, wgu_ref, wd_ref = refs
    else:
        x_ref, g_ref, wgu32_ref, wd32_ref, fg_ref, o_ref, wgu_ref, wd_ref = refs
        m_ref = mw_ref = None
    step = pl.program_id(0)

    @pl.when(step < FFN_STAGE_STEPS)
    def _():
        gu_rows, d_rows = wgu32_ref.shape[0], wd32_ref.shape[0]
        wgu_ref[pl.ds(pl.multiple_of(step * gu_rows, gu_rows), gu_rows), :] = (
            wgu32_ref[...].astype(BF16))
        wd_ref[pl.ds(pl.multiple_of(step * d_rows, d_rows), d_rows), :] = (
            wd32_ref[...].astype(BF16))

    @pl.when(step >= FFN_STAGE_STEPS)
    def _():
        _ffn_tile(x_ref, m_ref, mw_ref, g_ref, wgu_ref, wd_ref, fg_ref, o_ref, final_norm)


def _ffn_tile(x_ref, m_ref, mw_ref, g_ref, wgu_ref, wd_ref, fg_ref, o_ref, final_norm):
    chunks = list(zip(FFN_CHUNK_BOUNDS[:-1], FFN_CHUNK_BOUNDS[1:]))
    n_sub = FFN_ROWS // FFN_SUB_ROWS

    def rows(i):
        return slice(i * FFN_SUB_ROWS, (i + 1) * FFN_SUB_ROWS)

    def normed(i):
        x = x_ref[rows(i), :]
        if m_ref is not None:
            x = x + _dot(m_ref[rows(i), :], mw_ref[...])
            o_ref[rows(i), :] = x
        return _rmsnorm(x, g_ref[...]).astype(BF16)

    def finish(i, acc):
        base = x_ref if m_ref is None else o_ref
        y = base[rows(i), :] + 0.5 * acc
        if final_norm:
            y = _rmsnorm(y, fg_ref[...])
        o_ref[rows(i), :] = y

    h = normed(0)
    done = None
    for i in range(n_sub):
        acc = None
        for c, (lo, hi) in enumerate(chunks):
            gate = _dot(h, wgu_ref[:, lo:hi])
            if c == 0:
                h_next = normed(i + 1) if i + 1 < n_sub else None
                if done is not None:
                    finish(*done)
            up = _dot(h, wgu_ref[:, D_FF + lo:D_FF + hi])
            a = (gate * _sigmoid(gate) * up).astype(BF16)
            part = _dot(a, wd_ref[lo:hi, :])
            acc = part if acc is None else acc + part
        done = (i, acc)
        h = h_next
    finish(*done)


def _ffn(x2d, g, w_gu_all, w_down_all, layer, final_g, final_norm, mixer=None):
    n = x2d.shape[0]
    stage = FFN_STAGE_STEPS
    last = stage - 1

    def row_map(i):
        return (jnp.maximum(i - stage, 0), 0)

    row = pl.BlockSpec((FFN_ROWS, D_MODEL), row_map)
    operands, specs = [x2d], [row]
    if mixer is not None:
        m2d, w_m = mixer
        operands += [m2d, w_m.astype(BF16)]
        specs += [pl.BlockSpec((FFN_ROWS, m2d.shape[1]), row_map), _resident(w_m.shape)]
    operands += [g.reshape(1, D_MODEL), w_gu_all, w_down_all, final_g.reshape(1, D_MODEL)]
    specs += [_resident((1, D_MODEL)),
              pl.BlockSpec((None, D_MODEL // stage, 2 * D_FF),
                           lambda i: (layer, jnp.minimum(i, last), 0)),
              pl.BlockSpec((None, D_FF // stage, D_MODEL),
                           lambda i: (layer, jnp.minimum(i, last), 0)),
              _resident((1, D_MODEL))]
    return pl.pallas_call(
        functools.partial(_ffn_kernel, final_norm=final_norm, mixer_proj=mixer is not None),
        grid=(stage + n // FFN_ROWS,),
        in_specs=specs,
        out_specs=row,
        out_shape=jax.ShapeDtypeStruct((n, D_MODEL), F32),
        scratch_shapes=[pltpu.VMEM((D_MODEL, 2 * D_FF), BF16),
                        pltpu.VMEM((D_FF, D_MODEL), BF16)],
        compiler_params=_params("arbitrary"),
        name="ffn",
    )(*operands)


def _log_sigmoid(z):
    return jnp.minimum(z, 0.0) - jnp.log1p(jnp.exp(-jnp.abs(z)))


def _cumsum_lanes(v):
    n = v.shape[-1]
    lane = lax.broadcasted_iota(jnp.int32, v.shape, v.ndim - 1)
    s = 1
    while s < n:
        v = v + jnp.where(lane >= s, pltpu.roll(v, s, v.ndim - 1), 0.0)
        s *= 2
    return v


def _split_bf16_pieces(f):
    p1 = f.astype(BF16).astype(F32)
    r1 = f - p1
    p2 = r1.astype(BF16).astype(F32)
    return p1, p2, r1 - p2


def _fox_proj_kernel(x_ref, g_ref, wqv_ref, wk_ref, place_ref, wft_ref, bf_ref,
                     ka_ref, qt_ref, vt_ref, carry_ref):
    rows = PROJ_ROWS
    dh = FOX_HEAD_DIM

    @pl.when(pl.program_id(1) == 0)
    def _():
        carry_ref[...] = jnp.zeros_like(carry_ref)

    h = _rmsnorm(x_ref[0], g_ref[...]).astype(BF16)
    log_f = _log_sigmoid(_dot_nt(wft_ref[...], h) + bf_ref[...])
    cum = _cumsum_lanes(log_f) + carry_ref[:, 0:1]
    carry_ref[...] = jnp.broadcast_to(cum[:, rows - 1:rows], carry_ref.shape)
    pieces = _split_bf16_pieces(cum * LOG2E)

    qv_t = _dot_nt(wqv_ref[...], h)
    sub_q = lax.broadcasted_iota(jnp.int32, (BF16_SUBLANES, rows), 0)
    sub_k = lax.broadcasted_iota(jnp.int32, (AUG_SLOTS, rows), 0)
    vt_tail = jnp.where(sub_q == 0, 1.0, 0.0).astype(BF16)
    q_scale = LOG2E * dh ** -0.5
    ka_aug = []
    for hh in range(FOX_HEADS):
        qt_ref[0, hh, 0:dh, :] = (qv_t[hh * dh:(hh + 1) * dh] * q_scale).astype(BF16)
        aug_q = jnp.where(sub_q < 2 * F_PIECES, 1.0, 0.0)
        aug_k = jnp.where(sub_k < F_PIECES, 1.0, 0.0)
        for j in range(F_PIECES):
            pj = pieces[j][hh:hh + 1]
            aug_q = jnp.where(sub_q == j, pj, aug_q)
            aug_k = jnp.where(sub_k == F_PIECES + j, -pj, aug_k)
        qt_ref[0, hh, dh:dh + BF16_SUBLANES, :] = aug_q.astype(BF16)
        qt_ref[0, hh, dh + BF16_SUBLANES:, :] = jnp.zeros(
            (HEAD_K - dh - BF16_SUBLANES, rows), BF16)
        vt_ref[0, hh, 0:dh, :] = qv_t[D_MODEL + hh * dh:D_MODEL + (hh + 1) * dh].astype(BF16)
        vt_ref[0, hh, dh:, :] = vt_tail
        ka_aug.append(aug_k)
    ka_tok = jnp.concatenate(ka_aug, axis=0).T.astype(BF16)
    k_wide = _dot(h, wk_ref[...]) + _dot(ka_tok, place_ref[...])
    for hh in range(FOX_HEADS):
        ka_ref[0, hh] = k_wide[:, hh * HEAD_K:(hh + 1) * HEAD_K].astype(BF16)


def _aug_placement():
    p = np.zeros((FOX_HEADS * AUG_SLOTS, FOX_HEADS * HEAD_K), np.float32)
    for hh in range(FOX_HEADS):
        for j in range(AUG_SLOTS):
            p[hh * AUG_SLOTS + j, hh * HEAD_K + FOX_HEAD_DIM + j] = 1.0
    return p


def _fox_block_spec(tile, index_map):
    return pl.BlockSpec((None, 1, FOX_HEADS) + tile, lambda *a: index_map(*a) + (0, 0, 0))


def _fox_proj(x, g, w_in, b_f):
    b, t, d = x.shape
    assert PROJ_ROWS == ATTN_BLOCK
    nb = t // PROJ_ROWS
    hk = FOX_HEADS * HEAD_K
    w_q, w_k, w_v, w_f = (w_in[:, :d], w_in[:, d:2 * d], w_in[:, 2 * d:3 * d], w_in[:, 3 * d:])
    w_qv_t = jnp.concatenate([w_q.astype(BF16), w_v.astype(BF16)], axis=1).T
    w_k_wide = jnp.pad(w_k.astype(BF16).reshape(d, FOX_HEADS, FOX_HEAD_DIM),
                       ((0, 0), (0, 0), (0, HEAD_K - FOX_HEAD_DIM))).reshape(d, hk)
    place = jnp.asarray(_aug_placement(), BF16)
    row = pl.BlockSpec((1, PROJ_ROWS, d), lambda bi, ti: (bi, ti, 0))
    return pl.pallas_call(
        _fox_proj_kernel,
        grid=(b, nb),
        in_specs=[row, _resident((1, d)), _resident((2 * d, d)), _resident((d, hk)),
                  _resident((FOX_HEADS * AUG_SLOTS, hk)), _resident((FOX_HEADS, d)),
                  _resident((FOX_HEADS, 1))],
        out_specs=[_fox_block_spec((PROJ_ROWS, HEAD_K), lambda bi, ti: (bi, ti)),
                   _fox_block_spec((HEAD_K, PROJ_ROWS), lambda bi, ti: (bi, ti)),
                   _fox_block_spec((V_ROWS, PROJ_ROWS), lambda bi, ti: (bi, ti))],
        out_shape=[jax.ShapeDtypeStruct((b, nb, FOX_HEADS, PROJ_ROWS, HEAD_K), BF16),
                   jax.ShapeDtypeStruct((b, nb, FOX_HEADS, HEAD_K, PROJ_ROWS), BF16),
                   jax.ShapeDtypeStruct((b, nb, FOX_HEADS, V_ROWS, PROJ_ROWS), BF16)],
        scratch_shapes=[pltpu.VMEM((FOX_HEADS, LANES), F32)],
        compiler_params=_params("arbitrary", "arbitrary"),
        name="fox_proj",
    )(x, g.reshape(1, d), w_qv_t, w_k_wide, place, w_f.T.astype(BF16),
      b_f.reshape(FOX_HEADS, 1))


def _fox_attn_kernel(qi_ref, kj_ref, ka_ref, qt_ref, vt_ref, o_ref, m_ref, acc_ref):
    step = pl.program_id(1)
    qi = qi_ref[step]
    first = kj_ref[step] * ATTN_KEY_BLOCKS
    blk = ATTN_BLOCK
    dh = FOX_HEAD_DIM

    @pl.when(first == 0)
    def _():
        m_ref[...] = jnp.full_like(m_ref, -jnp.inf)
        acc_ref[...] = jnp.zeros_like(acc_ref)

    def run(n_blocks, diag_last):
        if diag_last:
            key = lax.broadcasted_iota(jnp.int32, (blk, blk), 0)
            qry = lax.broadcasted_iota(jnp.int32, (blk, blk), 1)
            keep = key <= qry

        def scores(u):
            j, h = divmod(u, FOX_HEADS)
            s = _dot(ka_ref[j, h], qt_ref[0, h])
            if diag_last and j == n_blocks - 1:
                s = jnp.where(keep, s, -jnp.inf)
            m_prev = m_ref[h]
            m_new = jnp.maximum(m_prev, jnp.max(s, axis=0, keepdims=True))
            m_ref[h] = m_new
            return s, m_new, jnp.exp2(m_prev - m_new)

        units = n_blocks * FOX_HEADS
        ahead = [scores(u) for u in range(ATTN_LOOKAHEAD)]
        for u in range(units):
            j, h = divmod(u, FOX_HEADS)
            s, m_new, alpha = ahead.pop(0)
            if u + ATTN_LOOKAHEAD < units:
                ahead.append(scores(u + ATTN_LOOKAHEAD))
            p = jnp.exp2(s - m_new).astype(BF16)
            acc_ref[h] = alpha * acc_ref[h] + _dot(vt_ref[j, h], p)

    def finalize():
        for hp in range(FOX_HEADS // 2):
            pair = []
            for h in (2 * hp, 2 * hp + 1):
                a = acc_ref[h]
                pair.append(a[0:dh] * (1.0 / a[dh:dh + 1]))
            o_ref[0, :, hp * LANES:(hp + 1) * LANES] = (
                jnp.concatenate(pair, axis=0).T.astype(BF16))

    @pl.when(first + ATTN_KEY_BLOCKS - 1 < qi)
    def _():
        run(ATTN_KEY_BLOCKS, False)

    for n in range(1, ATTN_KEY_BLOCKS + 1):
        @pl.when(first + n - 1 == qi)
        def _(n=n):
            run(n, True)
            finalize()


def _fox_attn(ka, qt, vt):
    b, nb = ka.shape[:2]
    t = nb * ATTN_BLOCK
    assert nb % ATTN_KEY_BLOCKS == 0
    groups = [-(-(i + 1) // ATTN_KEY_BLOCKS) for i in range(nb)]
    qi_tab = np.concatenate([np.full(g, i) for i, g in enumerate(groups)]).astype(np.int32)
    kj_tab = np.concatenate([np.arange(g) for g in groups]).astype(np.int32)
    o_spec = pl.BlockSpec((1, ATTN_BLOCK, D_MODEL), lambda bi, s, qi, kj: (bi, qi[s], 0))

    def key_spec(tile):
        return pl.BlockSpec((None, ATTN_KEY_BLOCKS, FOX_HEADS) + tile,
                            lambda bi, s, qi, kj: (bi, kj[s], 0, 0, 0))

    grid_spec = pltpu.PrefetchScalarGridSpec(
        num_scalar_prefetch=2,
        grid=(b, len(qi_tab)),
        in_specs=[key_spec((ATTN_BLOCK, HEAD_K)),
                  _fox_block_spec((HEAD_K, ATTN_BLOCK), lambda bi, s, qi, kj: (bi, qi[s])),
                  key_spec((V_ROWS, ATTN_BLOCK))],
        out_specs=o_spec,
        scratch_shapes=[pltpu.VMEM((FOX_HEADS, 1, ATTN_BLOCK), F32),
                        pltpu.VMEM((FOX_HEADS, V_ROWS, ATTN_BLOCK), F32)],
    )
    return pl.pallas_call(
        _fox_attn_kernel,
        grid_spec=grid_spec,
        out_shape=jax.ShapeDtypeStruct((b, t, D_MODEL), BF16),
        compiler_params=_params("arbitrary", "arbitrary"),
        name="fox_attn",
    )(jnp.asarray(qi_tab), jnp.asarray(kj_tab), ka, qt, vt)


def _gelu_tanh(x):
    k1 = float(2.0 * np.sqrt(2.0 / np.pi))
    return x * _sigmoid(x * (k1 + (k1 * 0.044715) * (x * x)))


def _softplus(z):
    return jnp.maximum(z, 0.0) + jnp.log1p(jnp.exp(-jnp.abs(z)))


def _linear_scan_rows(a, b, h0):
    rows, n = a.shape
    groups = rows // SUBLANES
    a = a.reshape(groups, SUBLANES, n)
    b = b.reshape(groups, SUBLANES, n)
    sub = lax.broadcasted_iota(jnp.int32, a.shape, 1)
    s = 1
    while s < SUBLANES:
        keep = sub >= s
        b = jnp.where(keep, a * pltpu.roll(b, s, 1) + b, b)
        a = jnp.where(keep, a * pltpu.roll(a, s, 1), a)
        s *= 2
    out = []
    carry = h0
    for g in range(groups):
        hg = b[g] + a[g] * carry
        out.append(hg)
        carry = hg[SUBLANES - 1:SUBLANES, :]
    return jnp.concatenate(out, axis=0)


def _lru_kernel(x_ref, g_ref, win_ref, cw_ref, cb_ref, wg_ref, ba_ref, bi_ref, lam_ref,
                wout_ref, y_ref, ext_ref, h_ref):
    rows = LRU_ROWS
    r = RNN_WIDTH

    @pl.when(pl.program_id(1) == 0)
    def _():
        ext_ref[...] = jnp.zeros_like(ext_ref)
        h_ref[...] = jnp.zeros_like(h_ref)

    gw = LRU_GROUP_W
    x = x_ref[0]
    h = _rmsnorm(x, g_ref[...]).astype(BF16)

    gr = _dot(h, win_ref[...])

    def recurrence(c):
        lanes = slice(c * gw, (c + 1) * gw)
        gate = _gelu_tanh(gr[:, c * gw:(c + 1) * gw])
        rec = gr[:, r + c * gw:r + (c + 1) * gw]
        ext = jnp.concatenate([ext_ref[:, lanes], rec], axis=0)
        ext_ref[:, lanes] = rec[rows - SUBLANES:, :]
        xc = cb_ref[:, lanes]
        for j in range(CONV_WIDTH):
            lag = CONV_WIDTH - 1 - j
            past = pltpu.roll(ext, lag, 0)[SUBLANES:, :] if lag else rec
            xc = xc + past * cw_ref[j:j + 1, lanes]
        both = _dot(xc.astype(BF16), wg_ref[c])
        rg = _sigmoid(both[:, :gw] + ba_ref[:, lanes])
        ig = _sigmoid(both[:, gw:] + bi_ref[:, lanes])
        log_a = (-RG_C * _softplus(-lam_ref[:, lanes])) * rg
        a = jnp.exp(log_a)
        th = jnp.tanh(log_a)
        w = -2.0 * th / (1.0 - th)
        bb = jnp.where(w > 0.0, w * lax.rsqrt(w), 0.0) * (ig * xc)
        hs = _linear_scan_rows(a, bb, h_ref[0:1, lanes])
        h_ref[:, lanes] = jnp.broadcast_to(hs[rows - 1:rows, :], (SUBLANES, gw))
        return (hs * gate).astype(BF16)

    y = jnp.concatenate([recurrence(c) for c in range(LRU_GROUPS)], axis=1)
    y_ref[0] = x + _dot(y, wout_ref[...])


def _block_diag_groups(w):
    per = RNN_BLOCKS // LRU_GROUPS
    w = w.reshape(LRU_GROUPS, per, RNN_BLOCK_W, 1, RNN_BLOCK_W)
    on_diag = jnp.eye(per, dtype=bool).reshape(1, per, 1, per, 1)
    return jnp.where(on_diag, w, 0.0).reshape(LRU_GROUPS, LRU_GROUP_W, LRU_GROUP_W)


def _lru(x, g, w_in, conv_w, conv_b, w_a, b_a, w_i, b_i, lam, w_out):
    b, t, _ = x.shape
    r = RNN_WIDTH
    w_gates = jnp.concatenate([_block_diag_groups(w_a), _block_diag_groups(w_i)],
                              axis=2).astype(BF16)
    row = pl.BlockSpec((1, LRU_ROWS, D_MODEL), lambda bi, ti: (bi, ti, 0))
    vec = _resident((1, r))
    return pl.pallas_call(
        _lru_kernel,
        grid=(b, t // LRU_ROWS),
        in_specs=[row, _resident((1, D_MODEL)), _resident((D_MODEL, 2 * r)),
                  _resident((CONV_WIDTH, r)), vec,
                  _resident((LRU_GROUPS, LRU_GROUP_W, 2 * LRU_GROUP_W)), vec, vec, vec,
                  _resident((r, D_MODEL))],
        out_specs=row,
        out_shape=jax.ShapeDtypeStruct((b, t, D_MODEL), F32),
        scratch_shapes=[pltpu.VMEM((SUBLANES, r), F32),
                        pltpu.VMEM((SUBLANES, r), F32)],
        compiler_params=_params("arbitrary", "arbitrary"),
        name="rglru",
    )(x, g.reshape(1, D_MODEL), w_in.astype(BF16), conv_w, conv_b.reshape(1, r), w_gates,
      b_a.reshape(1, r), b_i.reshape(1, r), lam.reshape(1, r), w_out.astype(BF16))


def kernel(x, ffn1_norm, ffn1_w_gu, ffn1_w_down, mix_norm, ffn2_norm, ffn2_w_gu, ffn2_w_down,
           fox_w_in, fox_b_f, fox_w_out, lru_w_in, lru_conv_w, lru_conv_b, lru_w_a, lru_b_a,
           lru_w_i, lru_b_i, lru_lambda, lru_w_out, final_norm):
    b, t, d = x.shape
    n = b * t
    for i in range(DEPTH):
        x = _ffn(x.reshape(n, d), ffn1_norm[i], ffn1_w_gu, ffn1_w_down, i, final_norm,
                 False).reshape(b, t, d)
        j = i // N_MIXERS
        mixer = None
        if i % N_MIXERS == 0:
            ka, qt, vt = _fox_proj(x, mix_norm[i], fox_w_in[j], fox_b_f[j])
            mixer = (_fox_attn(ka, qt, vt).reshape(n, d), fox_w_out[j])
        else:
            x = _lru(x, mix_norm[i], lru_w_in[j], lru_conv_w[j], lru_conv_b[j], lru_w_a[j],
                     lru_b_a[j], lru_w_i[j], lru_b_i[j], lru_lambda[j], lru_w_out[j])
        x = _ffn(x.reshape(n, d), ffn2_norm[i], ffn2_w_gu, ffn2_w_down, i, final_norm,
                 i == DEPTH - 1, mixer).reshape(b, t, d)
    return x
```

```python
import functools

import jax
import jax.numpy as jnp
import numpy as np
from jax import lax
from jax.experimental import pallas as pl
from jax.experimental.pallas import tpu as pltpu

D_MODEL = 1024
DEPTH = 2
N_MIXERS = 2
FOX_HEADS = 16
FOX_HEAD_DIM = D_MODEL // FOX_HEADS
RNN_WIDTH = 1280
RNN_BLOCKS = 16
RNN_BLOCK_W = RNN_WIDTH // RNN_BLOCKS
CONV_WIDTH = 4
RG_C = 8.0
D_FF = 2816
RMS_EPS = 1e-6
LOG2E = float(np.log2(np.e))

LANES = 128
SUBLANES = 8
BF16_SUBLANES = 16
VMEM_LIMIT_BYTES = 56 * 1024 * 1024

FFN_ROWS = 1024
FFN_SUB_ROWS = 512
FFN_STAGE_STEPS = 16
MXU_TILE = 256
FFN_CHUNK_BOUNDS = (0, 6 * MXU_TILE, D_FF)
PROJ_ROWS = 512
ATTN_BLOCK = 512
ATTN_KEY_BLOCKS = 2
ATTN_UNIT_KEYS = 256
ATTN_LOOKAHEAD = 2
LRU_ROWS = 256
LRU_GROUPS = 2
LRU_GROUP_W = RNN_WIDTH // LRU_GROUPS

HEAD_K = LANES
AUG_SLOTS = SUBLANES
F_PIECES = 3
V_ROWS = FOX_HEAD_DIM + BF16_SUBLANES

BF16 = jnp.bfloat16
F32 = jnp.float32


def _params(*sem):
    return pltpu.CompilerParams(dimension_semantics=sem, vmem_limit_bytes=VMEM_LIMIT_BYTES)


def _resident(shape):
    nd = len(shape)
    return pl.BlockSpec(shape, lambda *_: (0,) * nd, pipeline_mode=pl.Buffered(1))


def _rmsnorm(x, g):
    ms = jnp.mean(x * x, axis=-1, keepdims=True)
    return x * lax.rsqrt(ms + RMS_EPS) * g


def _dot(a, b):
    return jnp.dot(a, b, preferred_element_type=F32)


def _dot_nt(a, b):
    return lax.dot_general(a, b, (((1,), (1,)), ((), ())), preferred_element_type=F32)


def _sigmoid(x):
    return 1.0 / (1.0 + jnp.exp2(x * (-LOG2E)))


def _ffn_kernel(*refs, final_norm, mixer_proj):
    if mixer_proj:
        x_ref, m_ref, mw_ref, g_ref, wgu32_ref, wd32_ref, fg_ref, o_ref, wgu_ref, wd_ref = refs
    else:
        x_ref, g_ref, wgu32_ref, wd32_ref, fg_ref, o_ref, wgu_ref, wd_ref = refs
        m_ref = mw_ref = None
    step = pl.program_id(0)

    @pl.when(step < FFN_STAGE_STEPS)
    def _():
        gu_rows, d_rows = wgu32_ref.shape[0], wd32_ref.shape[0]
        wgu_ref[pl.ds(pl.multiple_of(step * gu_rows, gu_rows), gu_rows), :] = (
            wgu32_ref[...].astype(BF16))
        wd_ref[pl.ds(pl.multiple_of(step * d_rows, d_rows), d_rows), :] = (
            wd32_ref[...].astype(BF16))

    @pl.when(step >= FFN_STAGE_STEPS)
    def _():
        _ffn_tile(x_ref, m_ref, mw_ref, g_ref, wgu_ref, wd_ref, fg_ref, o_ref, final_norm)


def _ffn_tile(x_ref, m_ref, mw_ref, g_ref, wgu_ref, wd_ref, fg_ref, o_ref, final_norm):
    chunks = list(zip(FFN_CHUNK_BOUNDS[:-1], FFN_CHUNK_BOUNDS[1:]))
    n_sub = FFN_ROWS // FFN_SUB_ROWS

    def rows(i):
        return slice(i * FFN_SUB_ROWS, (i + 1) * FFN_SUB_ROWS)

    def normed(i):
        x = x_ref[rows(i), :]
        if m_ref is not None:
            x = x + _dot(m_ref[rows(i), :], mw_ref[...])
            o_ref[rows(i), :] = x
        return _rmsnorm(x, g_ref[...]).astype(BF16)

    def finish(i, acc):
        base = x_ref if m_ref is None else o_ref
        y = base[rows(i), :] + 0.5 * acc
        if final_norm:
            y = _rmsnorm(y, fg_ref[...])
        o_ref[rows(i), :] = y

    h = normed(0)
    done = None
    for i in range(n_sub):
        acc = None
        for c, (lo, hi) in enumerate(chunks):
            gate = _dot(h, wgu_ref[:, lo:hi])
            if c == 0:
                h_next = normed(i + 1) if i + 1 < n_sub else None
                if done is not None:
                    finish(*done)
            up = _dot(h, wgu_ref[:, D_FF + lo:D_FF + hi])
            a = (gate * _sigmoid(gate) * up).astype(BF16)
            part = _dot(a, wd_ref[lo:hi, :])
            acc = part if acc is None else acc + part
        done = (i, acc)
        h = h_next
    finish(*done)


def _ffn(x2d, g, w_gu_all, w_down_all, layer, final_g, final_norm, mixer=None):
    n = x2d.shape[0]
    stage = FFN_STAGE_STEPS
    last = stage - 1

    def row_map(i):
        return (jnp.maximum(i - stage, 0), 0)

    row = pl.BlockSpec((FFN_ROWS, D_MODEL), row_map)
    operands, specs = [x2d], [row]
    if mixer is not None:
        m2d, w_m = mixer
        operands += [m2d, w_m.astype(BF16)]
        specs += [pl.BlockSpec((FFN_ROWS, m2d.shape[1]), row_map), _resident(w_m.shape)]
    operands += [g.reshape(1, D_MODEL), w_gu_all, w_down_all, final_g.reshape(1, D_MODEL)]
    specs += [_resident((1, D_MODEL)),
              pl.BlockSpec((None, D_MODEL // stage, 2 * D_FF),
                           lambda i: (layer, jnp.minimum(i, last), 0)),
              pl.BlockSpec((None, D_FF // stage, D_MODEL),
                           lambda i: (layer, jnp.minimum(i, last), 0)),
              _resident((1, D_MODEL))]
    return pl.pallas_call(
        functools.partial(_ffn_kernel, final_norm=final_norm, mixer_proj=mixer is not None),
        grid=(stage + n // FFN_ROWS,),
        in_specs=specs,
        out_specs=row,
        out_shape=jax.ShapeDtypeStruct((n, D_MODEL), F32),
        scratch_shapes=[pltpu.VMEM((D_MODEL, 2 * D_FF), BF16),
                        pltpu.VMEM((D_FF, D_MODEL), BF16)],
        compiler_params=_params("arbitrary"),
        name="ffn",
    )(*operands)


def _log_sigmoid(z):
    return jnp.minimum(z, 0.0) - jnp.log1p(jnp.exp(-jnp.abs(z)))


def _cumsum_lanes(v):
    n = v.shape[-1]
    lane = lax.broadcasted_iota(jnp.int32, v.shape, v.ndim - 1)
    s = 1
    while s < n:
        v = v + jnp.where(lane >= s, pltpu.roll(v, s, v.ndim - 1), 0.0)
        s *= 2
    return v


def _split_bf16_pieces(f):
    p1 = f.astype(BF16).astype(F32)
    r1 = f - p1
    p2 = r1.astype(BF16).astype(F32)
    return p1, p2, r1 - p2


def _fox_proj_kernel(x_ref, g_ref, wqv_ref, wk_ref, place_ref, wft_ref, bf_ref,
                     ka_ref, qt_ref, vt_ref, carry_ref):
    rows = PROJ_ROWS
    dh = FOX_HEAD_DIM

    @pl.when(pl.program_id(1) == 0)
    def _():
        carry_ref[...] = jnp.zeros_like(carry_ref)

    h = _rmsnorm(x_ref[0], g_ref[...]).astype(BF16)
    log_f = _log_sigmoid(_dot_nt(wft_ref[...], h) + bf_ref[...])
    cum = _cumsum_lanes(log_f) + carry_ref[:, 0:1]
    carry_ref[...] = jnp.broadcast_to(cum[:, rows - 1:rows], carry_ref.shape)
    pieces = _split_bf16_pieces(cum * LOG2E)

    qv_t = _dot_nt(wqv_ref[...], h)
    sub_q = lax.broadcasted_iota(jnp.int32, (BF16_SUBLANES, rows), 0)
    sub_k = lax.broadcasted_iota(jnp.int32, (AUG_SLOTS, rows), 0)
    vt_tail = jnp.where(sub_q == 0, 1.0, 0.0).astype(BF16)
    q_scale = LOG2E * dh ** -0.5
    ka_aug = []
    for hh in range(FOX_HEADS):
        qt_ref[0, hh, 0:dh, :] = (qv_t[hh * dh:(hh + 1) * dh] * q_scale).astype(BF16)
        aug_q = jnp.where(sub_q < 2 * F_PIECES, 1.0, 0.0)
        aug_k = jnp.where(sub_k < F_PIECES, 1.0, 0.0)
        for j in range(F_PIECES):
            pj = pieces[j][hh:hh + 1]
            aug_q = jnp.where(sub_q == j, pj, aug_q)
            aug_k = jnp.where(sub_k == F_PIECES + j, -pj, aug_k)
        qt_ref[0, hh, dh:dh + BF16_SUBLANES, :] = aug_q.astype(BF16)
        qt_ref[0, hh, dh + BF16_SUBLANES:, :] = jnp.zeros(
            (HEAD_K - dh - BF16_SUBLANES, rows), BF16)
        vt_ref[0, hh, 0:dh, :] = qv_t[D_MODEL + hh * dh:D_MODEL + (hh + 1) * dh].astype(BF16)
        vt_ref[0, hh, dh:, :] = vt_tail
        ka_aug.append(aug_k)
    ka_tok = jnp.concatenate(ka_aug, axis=0).T.astype(BF16)
    k_wide = _dot(h, wk_ref[...]) + _dot(ka_tok, place_ref[...])
    for hh in range(FOX_HEADS):
        ka_ref[0, hh] = k_wide[:, hh * HEAD_K:(hh + 1) * HEAD_K].astype(BF16)


def _aug_placement():
    p = np.zeros((FOX_HEADS * AUG_SLOTS, FOX_HEADS * HEAD_K), np.float32)
    for hh in range(FOX_HEADS):
        for j in range(AUG_SLOTS):
            p[hh * AUG_SLOTS + j, hh * HEAD_K + FOX_HEAD_DIM + j] = 1.0
    return p


def _fox_block_spec(tile, index_map):
    return pl.BlockSpec((None, 1, FOX_HEADS) + tile, lambda *a: index_map(*a) + (0, 0, 0))


def _fox_proj(x, g, w_in, b_f):
    b, t, d = x.shape
    assert PROJ_ROWS == ATTN_BLOCK
    nb = t // PROJ_ROWS
    hk = FOX_HEADS * HEAD_K
    w_q, w_k, w_v, w_f = (w_in[:, :d], w_in[:, d:2 * d], w_in[:, 2 * d:3 * d], w_in[:, 3 * d:])
    w_qv_t = jnp.concatenate([w_q.astype(BF16), w_v.astype(BF16)], axis=1).T
    w_k_wide = jnp.pad(w_k.astype(BF16).reshape(d, FOX_HEADS, FOX_HEAD_DIM),
                       ((0, 0), (0, 0), (0, HEAD_K - FOX_HEAD_DIM))).reshape(d, hk)
    place = jnp.asarray(_aug_placement(), BF16)
    row = pl.BlockSpec((1, PROJ_ROWS, d), lambda bi, ti: (bi, ti, 0))
    return pl.pallas_call(
        _fox_proj_kernel,
        grid=(b, nb),
        in_specs=[row, _resident((1, d)), _resident((2 * d, d)), _resident((d, hk)),
                  _resident((FOX_HEADS * AUG_SLOTS, hk)), _resident((FOX_HEADS, d)),
                  _resident((FOX_HEADS, 1))],
        out_specs=[_fox_block_spec((PROJ_ROWS, HEAD_K), lambda bi, ti: (bi, ti)),
                   _fox_block_spec((HEAD_K, PROJ_ROWS), lambda bi, ti: (bi, ti)),
                   _fox_block_spec((V_ROWS, PROJ_ROWS), lambda bi, ti: (bi, ti))],
        out_shape=[jax.ShapeDtypeStruct((b, nb, FOX_HEADS, PROJ_ROWS, HEAD_K), BF16),
                   jax.ShapeDtypeStruct((b, nb, FOX_HEADS, HEAD_K, PROJ_ROWS), BF16),
                   jax.ShapeDtypeStruct((b, nb, FOX_HEADS, V_ROWS, PROJ_ROWS), BF16)],
        scratch_shapes=[pltpu.VMEM((FOX_HEADS, LANES), F32)],
        compiler_params=_params("arbitrary", "arbitrary"),
        name="fox_proj",
    )(x, g.reshape(1, d), w_qv_t, w_k_wide, place, w_f.T.astype(BF16),
      b_f.reshape(FOX_HEADS, 1))


def _fox_attn_kernel(qi_ref, kj_ref, ka_ref, qt_ref, vt_ref, o_ref, m_ref, acc_ref):
    step = pl.program_id(1)
    qi = qi_ref[step]
    first = kj_ref[step] * ATTN_KEY_BLOCKS
    blk = ATTN_BLOCK
    dh = FOX_HEAD_DIM

    @pl.when(first == 0)
    def _():
        m_ref[...] = jnp.full_like(m_ref, -jnp.inf)
        acc_ref[...] = jnp.zeros_like(acc_ref)

    def run(n_blocks, diag_last):
        uk = ATTN_UNIT_KEYS
        parts = blk // uk
        if diag_last:
            key = lax.broadcasted_iota(jnp.int32, (uk, blk), 0)
            qry = lax.broadcasted_iota(jnp.int32, (uk, blk), 1)

        def split(u):
            jp, h = divmod(u, FOX_HEADS)
            j, part = divmod(jp, parts)
            return j, part, h

        def scores(u):
            j, part, h = split(u)
            s = _dot(ka_ref[j, h, part * uk:(part + 1) * uk, :], qt_ref[0, h])
            if diag_last and j == n_blocks - 1:
                s = jnp.where(key + part * uk <= qry, s, -jnp.inf)
            m_prev = m_ref[h]
            m_new = jnp.maximum(m_prev, jnp.max(s, axis=0, keepdims=True))
            m_ref[h] = m_new
            return s, m_new, jnp.exp2(m_prev - m_new)

        units = n_blocks * parts * FOX_HEADS
        ahead = [scores(u) for u in range(ATTN_LOOKAHEAD)]
        for u in range(units):
            j, part, h = split(u)
            s, m_new, alpha = ahead.pop(0)
            if u + ATTN_LOOKAHEAD < units:
                ahead.append(scores(u + ATTN_LOOKAHEAD))
            p = jnp.exp2(s - m_new).astype(BF16)
            pv = _dot(vt_ref[j, h, :, part * uk:(part + 1) * uk], p)
            acc_ref[h] = alpha * acc_ref[h] + pv

    def finalize():
        for hp in range(FOX_HEADS // 2):
            pair = []
            for h in (2 * hp, 2 * hp + 1):
                a = acc_ref[h]
                pair.append(a[0:dh] * (1.0 / a[dh:dh + 1]))
            o_ref[0, :, hp * LANES:(hp + 1) * LANES] = (
                jnp.concatenate(pair, axis=0).T.astype(BF16))

    @pl.when(first + ATTN_KEY_BLOCKS - 1 < qi)
    def _():
        run(ATTN_KEY_BLOCKS, False)

    for n in range(1, ATTN_KEY_BLOCKS + 1):
        @pl.when(first + n - 1 == qi)
        def _(n=n):
            run(n, True)
            finalize()


def _fox_attn(ka, qt, vt):
    b, nb = ka.shape[:2]
    t = nb * ATTN_BLOCK
    assert nb % ATTN_KEY_BLOCKS == 0
    groups = [-(-(i + 1) // ATTN_KEY_BLOCKS) for i in range(nb)]
    qi_tab = np.concatenate([np.full(g, i) for i, g in enumerate(groups)]).astype(np.int32)
    kj_tab = np.concatenate([np.arange(g) for g in groups]).astype(np.int32)
    o_spec = pl.BlockSpec((1, ATTN_BLOCK, D_MODEL), lambda bi, s, qi, kj: (bi, qi[s], 0))

    def key_spec(tile):
        return pl.BlockSpec((None, ATTN_KEY_BLOCKS, FOX_HEADS) + tile,
                            lambda bi, s, qi, kj: (bi, kj[s], 0, 0, 0))

    grid_spec = pltpu.PrefetchScalarGridSpec(
        num_scalar_prefetch=2,
        grid=(b, len(qi_tab)),
        in_specs=[key_spec((ATTN_BLOCK, HEAD_K)),
                  _fox_block_spec((HEAD_K, ATTN_BLOCK), lambda bi, s, qi, kj: (bi, qi[s])),
                  key_spec((V_ROWS, ATTN_BLOCK))],
        out_specs=o_spec,
        scratch_shapes=[pltpu.VMEM((FOX_HEADS, 1, ATTN_BLOCK), F32),
                        pltpu.VMEM((FOX_HEADS, V_ROWS, ATTN_BLOCK), F32)],
    )
    return pl.pallas_call(
        _fox_attn_kernel,
        grid_spec=grid_spec,
        out_shape=jax.ShapeDtypeStruct((b, t, D_MODEL), BF16),
        compiler_params=_params("arbitrary", "arbitrary"),
        name="fox_attn",
    )(jnp.asarray(qi_tab), jnp.asarray(kj_tab), ka, qt, vt)


def _gelu_tanh(x):
    k1 = float(2.0 * np.sqrt(2.0 / np.pi))
    return x * _sigmoid(x * (k1 + (k1 * 0.044715) * (x * x)))


def _softplus(z):
    return jnp.maximum(z, 0.0) + jnp.log1p(jnp.exp(-jnp.abs(z)))


def _linear_scan_rows(a, b, h0):
    rows, n = a.shape
    groups = rows // SUBLANES
    a = a.reshape(groups, SUBLANES, n)
    b = b.reshape(groups, SUBLANES, n)
    sub = lax.broadcasted_iota(jnp.int32, a.shape, 1)
    s = 1
    while s < SUBLANES:
        keep = sub >= s
        b = jnp.where(keep, a * pltpu.roll(b, s, 1) + b, b)
        a = jnp.where(keep, a * pltpu.roll(a, s, 1), a)
        s *= 2
    out = []
    carry = h0
    for g in range(groups):
        hg = b[g] + a[g] * carry
        out.append(hg)
        carry = hg[SUBLANES - 1:SUBLANES, :]
    return jnp.concatenate(out, axis=0)


def _lru_kernel(x_ref, g_ref, win_ref, cw_ref, cb_ref, wg_ref, ba_ref, bi_ref, lam_ref,
                wout_ref, y_ref, ext_ref, h_ref):
    rows = LRU_ROWS
    r = RNN_WIDTH

    @pl.when(pl.program_id(1) == 0)
    def _():
        ext_ref[...] = jnp.zeros_like(ext_ref)
        h_ref[...] = jnp.zeros_like(h_ref)

    gw = LRU_GROUP_W
    x = x_ref[0]
    h = _rmsnorm(x, g_ref[...]).astype(BF16)

    gr = _dot(h, win_ref[...])

    def recurrence(c):
        lanes = slice(c * gw, (c + 1) * gw)
        gate = _gelu_tanh(gr[:, c * gw:(c + 1) * gw])
        rec = gr[:, r + c * gw:r + (c + 1) * gw]
        ext = jnp.concatenate([ext_ref[:, lanes], rec], axis=0)
        ext_ref[:, lanes] = rec[rows - SUBLANES:, :]
        xc = cb_ref[:, lanes]
        for j in range(CONV_WIDTH):
            lag = CONV_WIDTH - 1 - j
            past = pltpu.roll(ext, lag, 0)[SUBLANES:, :] if lag else rec
            xc = xc + past * cw_ref[j:j + 1, lanes]
        both = _dot(xc.astype(BF16), wg_ref[c])
        rg = _sigmoid(both[:, :gw] + ba_ref[:, lanes])
        ig = _sigmoid(both[:, gw:] + bi_ref[:, lanes])
        log_a = (-RG_C * _softplus(-lam_ref[:, lanes])) * rg
        a = jnp.exp(log_a)
        th = jnp.tanh(log_a)
        w = -2.0 * th / (1.0 - th)
        bb = jnp.where(w > 0.0, w * lax.rsqrt(w), 0.0) * (ig * xc)
        hs = _linear_scan_rows(a, bb, h_ref[0:1, lanes])
        h_ref[:, lanes] = jnp.broadcast_to(hs[rows - 1:rows, :], (SUBLANES, gw))
        return (hs * gate).astype(BF16)

    y = jnp.concatenate([recurrence(c) for c in range(LRU_GROUPS)], axis=1)
    y_ref[0] = x + _dot(y, wout_ref[...])


def _block_diag_groups(w):
    per = RNN_BLOCKS // LRU_GROUPS
    w = w.reshape(LRU_GROUPS, per, RNN_BLOCK_W, 1, RNN_BLOCK_W)
    on_diag = jnp.eye(per, dtype=bool).reshape(1, per, 1, per, 1)
    return jnp.where(on_diag, w, 0.0).reshape(LRU_GROUPS, LRU_GROUP_W, LRU_GROUP_W)


def _lru(x, g, w_in, conv_w, conv_b, w_a, b_a, w_i, b_i, lam, w_out):
    b, t, _ = x.shape
    r = RNN_WIDTH
    w_gates = jnp.concatenate([_block_diag_groups(w_a), _block_diag_groups(w_i)],
                              axis=2).astype(BF16)
    row = pl.BlockSpec((1, LRU_ROWS, D_MODEL), lambda bi, ti: (bi, ti, 0))
    vec = _resident((1, r))
    return pl.pallas_call(
        _lru_kernel,
        grid=(b, t // LRU_ROWS),
        in_specs=[row, _resident((1, D_MODEL)), _resident((D_MODEL, 2 * r)),
                  _resident((CONV_WIDTH, r)), vec,
                  _resident((LRU_GROUPS, LRU_GROUP_W, 2 * LRU_GROUP_W)), vec, vec, vec,
                  _resident((r, D_MODEL))],
        out_specs=row,
        out_shape=jax.ShapeDtypeStruct((b, t, D_MODEL), F32),
        scratch_shapes=[pltpu.VMEM((SUBLANES, r), F32),
                        pltpu.VMEM((SUBLANES, r), F32)],
        compiler_params=_params("arbitrary", "arbitrary"),
        name="rglru",
    )(x, g.reshape(1, D_MODEL), w_in.astype(BF16), conv_w, conv_b.reshape(1, r), w_gates,
      b_a.reshape(1, r), b_i.reshape(1, r), lam.reshape(1, r), w_out.astype(BF16))


def kernel(x, ffn1_norm, ffn1_w_gu, ffn1_w_down, mix_norm, ffn2_norm, ffn2_w_gu, ffn2_w_down,
           fox_w_in, fox_b_f, fox_w_out, lru_w_in, lru_conv_w, lru_conv_b, lru_w_a, lru_b_a,
           lru_w_i, lru_b_i, lru_lambda, lru_w_out, final_norm):
    b, t, d = x.shape
    n = b * t
    for i in range(DEPTH):
        x = _ffn(x.reshape(n, d), ffn1_norm[i], ffn1_w_gu, ffn1_w_down, i, final_norm,
                 False).reshape(b, t, d)
        j = i // N_MIXERS
        mixer = None
        if i % N_MIXERS == 0:
            ka, qt, vt = _fox_proj(x, mix_norm[i], fox_w_in[j], fox_b_f[j])
            mixer = (_fox_attn(ka, qt, vt).reshape(n, d), fox_w_out[j])
        else:
            x = _lru(x, mix_norm[i], lru_w_in[j], lru_conv_w[j], lru_conv_b[j], lru_w_a[j],
                     lru_b_a[j], lru_w_i[j], lru_b_i[j], lru_lambda[j], lru_w_out[j])
        x = _ffn(x.reshape(n, d), ffn2_norm[i], ffn2_w_gu, ffn2_w_down, i, final_norm,
                 i == DEPTH - 1, mixer).reshape(b, t, d)
    return x
```

```python
import functools

import jax
import jax.numpy as jnp
import numpy as np
from jax import lax
from jax.experimental import pallas as pl
from jax.experimental.pallas import tpu as pltpu

D_MODEL = 1024
DEPTH = 2
N_MIXERS = 2
FOX_HEADS = 16
FOX_HEAD_DIM = D_MODEL // FOX_HEADS
RNN_WIDTH = 1280
RNN_BLOCKS = 16
RNN_BLOCK_W = RNN_WIDTH // RNN_BLOCKS
CONV_WIDTH = 4
RG_C = 8.0
D_FF = 2816
RMS_EPS = 1e-6
LOG2E = float(np.log2(np.e))

LANES = 128
SUBLANES = 8
BF16_SUBLANES = 16
VMEM_LIMIT_BYTES = 56 * 1024 * 1024

FFN_ROWS = 1024
FFN_SUB_ROWS = 512
FFN_STAGE_STEPS = 16
MXU_TILE = 256
FFN_CHUNK_BOUNDS = (0, 6 * MXU_TILE, D_FF)
PROJ_ROWS = 512
ATTN_BLOCK = 512
ATTN_KEY_BLOCKS = 2
ATTN_UNIT_KEYS = 256
ATTN_LOOKAHEAD = 2
LRU_ROWS = 512
LRU_GROUPS = 2
LRU_GROUP_W = RNN_WIDTH // LRU_GROUPS

HEAD_K = LANES
AUG_SLOTS = SUBLANES
F_PIECES = 3
V_ROWS = FOX_HEAD_DIM + BF16_SUBLANES

BF16 = jnp.bfloat16
F32 = jnp.float32


def _params(*sem):
    return pltpu.CompilerParams(dimension_semantics=sem, vmem_limit_bytes=VMEM_LIMIT_BYTES)


def _resident(shape):
    nd = len(shape)
    return pl.BlockSpec(shape, lambda *_: (0,) * nd, pipeline_mode=pl.Buffered(1))


def _rmsnorm(x, g):
    ms = jnp.mean(x * x, axis=-1, keepdims=True)
    return x * lax.rsqrt(ms + RMS_EPS) * g


def _dot(a, b):
    return jnp.dot(a, b, preferred_element_type=F32)


def _dot_nt(a, b):
    return lax.dot_general(a, b, (((1,), (1,)), ((), ())), preferred_element_type=F32)


def _sigmoid(x):
    return 1.0 / (1.0 + jnp.exp2(x * (-LOG2E)))


def _ffn_kernel(*refs, final_norm, mixer_proj):
    if mixer_proj:
        x_ref, m_ref, mw_ref, g_ref, wgu32_ref, wd32_ref, fg_ref, o_ref, wgu_ref, wd_ref = refs
    else:
        x_ref, g_ref, wgu32_ref, wd32_ref, fg_ref, o_ref, wgu_ref, wd_ref = refs
        m_ref = mw_ref = None
    step = pl.program_id(0)

    @pl.when(step < FFN_STAGE_STEPS)
    def _():
        gu_rows, d_rows = wgu32_ref.shape[0], wd32_ref.shape[0]
        wgu_ref[pl.ds(pl.multiple_of(step * gu_rows, gu_rows), gu_rows), :] = (
            wgu32_ref[...].astype(BF16))
        wd_ref[pl.ds(pl.multiple_of(step * d_rows, d_rows), d_rows), :] = (
            wd32_ref[...].astype(BF16))

    @pl.when(step >= FFN_STAGE_STEPS)
    def _():
        _ffn_tile(x_ref, m_ref, mw_ref, g_ref, wgu_ref, wd_ref, fg_ref, o_ref, final_norm)


def _ffn_tile(x_ref, m_ref, mw_ref, g_ref, wgu_ref, wd_ref, fg_ref, o_ref, final_norm):
    chunks = list(zip(FFN_CHUNK_BOUNDS[:-1], FFN_CHUNK_BOUNDS[1:]))
    n_sub = FFN_ROWS // FFN_SUB_ROWS

    def rows(i):
        return slice(i * FFN_SUB_ROWS, (i + 1) * FFN_SUB_ROWS)

    def normed(i):
        x = x_ref[rows(i), :]
        if m_ref is not None:
            x = x + _dot(m_ref[rows(i), :], mw_ref[...])
            o_ref[rows(i), :] = x
        return _rmsnorm(x, g_ref[...]).astype(BF16)

    def finish(i, acc):
        base = x_ref if m_ref is None else o_ref
        y = base[rows(i), :] + 0.5 * acc
        if final_norm:
            y = _rmsnorm(y, fg_ref[...])
        o_ref[rows(i), :] = y

    h = normed(0)
    done = None
    for i in range(n_sub):
        acc = None
        for c, (lo, hi) in enumerate(chunks):
            gate = _dot(h, wgu_ref[:, lo:hi])
            if c == 0:
                h_next = normed(i + 1) if i + 1 < n_sub else None
                if done is not None:
                    finish(*done)
            up = _dot(h, wgu_ref[:, D_FF + lo:D_FF + hi])
            a = (gate * _sigmoid(gate) * up).astype(BF16)
            part = _dot(a, wd_ref[lo:hi, :])
            acc = part if acc is None else acc + part
        done = (i, acc)
        h = h_next
    finish(*done)


def _ffn(x2d, g, w_gu_all, w_down_all, layer, final_g, final_norm, mixer=None):
    n = x2d.shape[0]
    stage = FFN_STAGE_STEPS
    last = stage - 1

    def row_map(i):
        return (jnp.maximum(i - stage, 0), 0)

    row = pl.BlockSpec((FFN_ROWS, D_MODEL), row_map)
    operands, specs = [x2d], [row]
    if mixer is not None:
        m2d, w_m = mixer
        operands += [m2d, w_m.astype(BF16)]
        specs += [pl.BlockSpec((FFN_ROWS, m2d.shape[1]), row_map), _resident(w_m.shape)]
    operands += [g.reshape(1, D_MODEL), w_gu_all, w_down_all, final_g.reshape(1, D_MODEL)]
    specs += [_resident((1, D_MODEL)),
              pl.BlockSpec((None, D_MODEL // stage, 2 * D_FF),
                           lambda i: (layer, jnp.minimum(i, last), 0)),
              pl.BlockSpec((None, D_FF // stage, D_MODEL),
                           lambda i: (layer, jnp.minimum(i, last), 0)),
              _resident((1, D_MODEL))]
    return pl.pallas_call(
        functools.partial(_ffn_kernel, final_norm=final_norm, mixer_proj=mixer is not None),
        grid=(stage + n // FFN_ROWS,),
        in_specs=specs,
        out_specs=row,
        out_shape=jax.ShapeDtypeStruct((n, D_MODEL), F32),
        scratch_shapes=[pltpu.VMEM((D_MODEL, 2 * D_FF), BF16),
                        pltpu.VMEM((D_FF, D_MODEL), BF16)],
        compiler_params=_params("arbitrary"),
        name="ffn",
    )(*operands)


def _log_sigmoid(z):
    return jnp.minimum(z, 0.0) - jnp.log1p(jnp.exp(-jnp.abs(z)))


def _cumsum_lanes(v):
    n = v.shape[-1]
    lane = lax.broadcasted_iota(jnp.int32, v.shape, v.ndim - 1)
    s = 1
    while s < n:
        v = v + jnp.where(lane >= s, pltpu.roll(v, s, v.ndim - 1), 0.0)
        s *= 2
    return v


def _split_bf16_pieces(f):
    p1 = f.astype(BF16).astype(F32)
    r1 = f - p1
    p2 = r1.astype(BF16).astype(F32)
    return p1, p2, r1 - p2


def _fox_proj_kernel(x_ref, g_ref, wqkv_ref, place_ref, wft_ref, bf_ref,
                     ka_ref, qt_ref, vt_ref, carry_ref):
    rows = PROJ_ROWS
    dh = FOX_HEAD_DIM

    @pl.when(pl.program_id(1) == 0)
    def _():
        carry_ref[...] = jnp.zeros_like(carry_ref)

    h = _rmsnorm(x_ref[0], g_ref[...]).astype(BF16)
    log_f = _log_sigmoid(_dot_nt(wft_ref[...], h) + bf_ref[...])
    cum = _cumsum_lanes(log_f) + carry_ref[:, 0:1]
    carry_ref[...] = jnp.broadcast_to(cum[:, rows - 1:rows], carry_ref.shape)
    pieces = _split_bf16_pieces(cum * LOG2E)

    qkv = _dot(h, wqkv_ref[...])
    sub_q = lax.broadcasted_iota(jnp.int32, (BF16_SUBLANES, rows), 0)
    sub_k = lax.broadcasted_iota(jnp.int32, (AUG_SLOTS, rows), 0)
    vt_tail = jnp.where(sub_q == 0, 1.0, 0.0).astype(BF16)
    q_scale = LOG2E * dh ** -0.5
    ka_aug = []
    for hh in range(FOX_HEADS):
        aug_q = jnp.where(sub_q < 2 * F_PIECES, 1.0, 0.0)
        aug_k = jnp.where(sub_k < F_PIECES, 1.0, 0.0)
        for j in range(F_PIECES):
            pj = pieces[j][hh:hh + 1]
            aug_q = jnp.where(sub_q == j, pj, aug_q)
            aug_k = jnp.where(sub_k == F_PIECES + j, -pj, aug_k)
        qt_ref[0, hh, dh:dh + BF16_SUBLANES, :] = aug_q.astype(BF16)
        qt_ref[0, hh, dh + BF16_SUBLANES:, :] = jnp.zeros(
            (HEAD_K - dh - BF16_SUBLANES, rows), BF16)
        vt_ref[0, hh, dh:, :] = vt_tail
        ka_aug.append(aug_k)
    ka_tok = jnp.concatenate(ka_aug, axis=0).T.astype(BF16)
    aug_wide = _dot(ka_tok, place_ref[...])
    lane = lax.broadcasted_iota(jnp.int32, (rows, LANES), 1)
    for pair in range(FOX_HEADS // 2):
        cols = slice(pair * LANES, (pair + 1) * LANES)
        q_t = (qkv[:, cols] * q_scale).T
        v_t = qkv[:, 2 * D_MODEL + pair * LANES:2 * D_MODEL + (pair + 1) * LANES].T
        k_even = qkv[:, D_MODEL + pair * LANES:D_MODEL + (pair + 1) * LANES]
        k_odd = pltpu.roll(k_even, dh, 1)
        for e, k_src in enumerate((k_even, k_odd)):
            hh = 2 * pair + e
            qt_ref[0, hh, 0:dh, :] = q_t[e * dh:(e + 1) * dh].astype(BF16)
            vt_ref[0, hh, 0:dh, :] = v_t[e * dh:(e + 1) * dh].astype(BF16)
            ka_ref[0, hh] = jnp.where(lane < dh, k_src,
                                      aug_wide[:, hh * HEAD_K:(hh + 1) * HEAD_K]).astype(BF16)


def _aug_placement():
    p = np.zeros((FOX_HEADS * AUG_SLOTS, FOX_HEADS * HEAD_K), np.float32)
    for hh in range(FOX_HEADS):
        for j in range(AUG_SLOTS):
            p[hh * AUG_SLOTS + j, hh * HEAD_K + FOX_HEAD_DIM + j] = 1.0
    return p


def _fox_block_spec(tile, index_map):
    return pl.BlockSpec((None, 1, FOX_HEADS) + tile, lambda *a: index_map(*a) + (0, 0, 0))


def _fox_proj(x, g, w_in, b_f):
    b, t, d = x.shape
    assert PROJ_ROWS == ATTN_BLOCK
    nb = t // PROJ_ROWS
    hk = FOX_HEADS * HEAD_K
    w_qkv = w_in[:, :3 * d].astype(BF16)
    w_f = w_in[:, 3 * d:]
    place = jnp.asarray(_aug_placement(), BF16)
    row = pl.BlockSpec((1, PROJ_ROWS, d), lambda bi, ti: (bi, ti, 0))
    return pl.pallas_call(
        _fox_proj_kernel,
        grid=(b, nb),
        in_specs=[row, _resident((1, d)), _resident((d, 3 * d)),
                  _resident((FOX_HEADS * AUG_SLOTS, hk)), _resident((FOX_HEADS, d)),
                  _resident((FOX_HEADS, 1))],
        out_specs=[_fox_block_spec((PROJ_ROWS, HEAD_K), lambda bi, ti: (bi, ti)),
                   _fox_block_spec((HEAD_K, PROJ_ROWS), lambda bi, ti: (bi, ti)),
                   _fox_block_spec((V_ROWS, PROJ_ROWS), lambda bi, ti: (bi, ti))],
        out_shape=[jax.ShapeDtypeStruct((b, nb, FOX_HEADS, PROJ_ROWS, HEAD_K), BF16),
                   jax.ShapeDtypeStruct((b, nb, FOX_HEADS, HEAD_K, PROJ_ROWS), BF16),
                   jax.ShapeDtypeStruct((b, nb, FOX_HEADS, V_ROWS, PROJ_ROWS), BF16)],
        scratch_shapes=[pltpu.VMEM((FOX_HEADS, LANES), F32)],
        compiler_params=_params("arbitrary", "arbitrary"),
        name="fox_proj",
    )(x, g.reshape(1, d), w_qkv, place, w_f.T.astype(BF16), b_f.reshape(FOX_HEADS, 1))


def _fox_attn_kernel(qi_ref, kj_ref, ka_ref, qt_ref, vt_ref, o_ref, m_ref, acc_ref):
    step = pl.program_id(1)
    qi = qi_ref[step]
    first = kj_ref[step] * ATTN_KEY_BLOCKS
    blk = ATTN_BLOCK
    dh = FOX_HEAD_DIM

    @pl.when(first == 0)
    def _():
        m_ref[...] = jnp.full_like(m_ref, -jnp.inf)
        acc_ref[...] = jnp.zeros_like(acc_ref)

    def run(n_blocks, diag_last):
        uk = ATTN_UNIT_KEYS
        parts = blk // uk
        if diag_last:
            key = lax.broadcasted_iota(jnp.int32, (uk, blk), 0)
            qry = lax.broadcasted_iota(jnp.int32, (uk, blk), 1)

        def split(u):
            jp, h = divmod(u, FOX_HEADS)
            j, part = divmod(jp, parts)
            return j, part, h

        def scores(u):
            j, part, h = split(u)
            s = _dot(ka_ref[j, h, part * uk:(part + 1) * uk, :], qt_ref[0, h])
            if diag_last and j == n_blocks - 1:
                s = jnp.where(key + part * uk <= qry, s, -jnp.inf)
            m_prev = m_ref[h]
            m_new = jnp.maximum(m_prev, jnp.max(s, axis=0, keepdims=True))
            m_ref[h] = m_new
            return s, m_new, jnp.exp2(m_prev - m_new)

        units = n_blocks * parts * FOX_HEADS
        ahead = [scores(u) for u in range(ATTN_LOOKAHEAD)]
        for u in range(units):
            j, part, h = split(u)
            s, m_new, alpha = ahead.pop(0)
            if u + ATTN_LOOKAHEAD < units:
                ahead.append(scores(u + ATTN_LOOKAHEAD))
            p = jnp.exp2(s - m_new).astype(BF16)
            pv = _dot(vt_ref[j, h, :, part * uk:(part + 1) * uk], p)
            acc_ref[h] = alpha * acc_ref[h] + pv

    def finalize():
        for hp in range(FOX_HEADS // 2):
            pair = []
            for h in (2 * hp, 2 * hp + 1):
                a = acc_ref[h]
                pair.append(a[0:dh] * (1.0 / a[dh:dh + 1]))
            o_ref[0, :, hp * LANES:(hp + 1) * LANES] = (
                jnp.concatenate(pair, axis=0).T.astype(BF16))

    @pl.when(first + ATTN_KEY_BLOCKS - 1 < qi)
    def _():
        run(ATTN_KEY_BLOCKS, False)

    for n in range(1, ATTN_KEY_BLOCKS + 1):
        @pl.when(first + n - 1 == qi)
        def _(n=n):
            run(n, True)
            finalize()


def _fox_attn(ka, qt, vt):
    b, nb = ka.shape[:2]
    t = nb * ATTN_BLOCK
    assert nb % ATTN_KEY_BLOCKS == 0
    groups = [-(-(i + 1) // ATTN_KEY_BLOCKS) for i in range(nb)]
    qi_tab = np.concatenate([np.full(g, i) for i, g in enumerate(groups)]).astype(np.int32)
    kj_tab = np.concatenate([np.arange(g) for g in groups]).astype(np.int32)
    o_spec = pl.BlockSpec((1, ATTN_BLOCK, D_MODEL), lambda bi, s, qi, kj: (bi, qi[s], 0))

    def key_spec(tile):
        return pl.BlockSpec((None, ATTN_KEY_BLOCKS, FOX_HEADS) + tile,
                            lambda bi, s, qi, kj: (bi, kj[s], 0, 0, 0))

    grid_spec = pltpu.PrefetchScalarGridSpec(
        num_scalar_prefetch=2,
        grid=(b, len(qi_tab)),
        in_specs=[key_spec((ATTN_BLOCK, HEAD_K)),
                  _fox_block_spec((HEAD_K, ATTN_BLOCK), lambda bi, s, qi, kj: (bi, qi[s])),
                  key_spec((V_ROWS, ATTN_BLOCK))],
        out_specs=o_spec,
        scratch_shapes=[pltpu.VMEM((FOX_HEADS, 1, ATTN_BLOCK), F32),
                        pltpu.VMEM((FOX_HEADS, V_ROWS, ATTN_BLOCK), F32)],
    )
    return pl.pallas_call(
        _fox_attn_kernel,
        grid_spec=grid_spec,
        out_shape=jax.ShapeDtypeStruct((b, t, D_MODEL), BF16),
        compiler_params=_params("arbitrary", "arbitrary"),
        name="fox_attn",
    )(jnp.asarray(qi_tab), jnp.asarray(kj_tab), ka, qt, vt)


def _gelu_tanh(x):
    k1 = float(2.0 * np.sqrt(2.0 / np.pi))
    return x * _sigmoid(x * (k1 + (k1 * 0.044715) * (x * x)))


def _softplus(z):
    return jnp.maximum(z, 0.0) + jnp.log1p(jnp.exp(-jnp.abs(z)))


def _linear_scan_rows(a, b, h0):
    rows, n = a.shape
    groups = rows // SUBLANES
    a = a.reshape(groups, SUBLANES, n)
    b = b.reshape(groups, SUBLANES, n)
    sub = lax.broadcasted_iota(jnp.int32, a.shape, 1)
    s = 1
    while s < SUBLANES:
        keep = sub >= s
        b = jnp.where(keep, a * pltpu.roll(b, s, 1) + b, b)
        a = jnp.where(keep, a * pltpu.roll(a, s, 1), a)
        s *= 2
    out = []
    carry = h0
    for g in range(groups):
        hg = b[g] + a[g] * carry
        out.append(hg)
        carry = hg[SUBLANES - 1:SUBLANES, :]
    return jnp.concatenate(out, axis=0)


def _lru_kernel(x_ref, g_ref, win_ref, cw_ref, cb_ref, wg_ref, ba_ref, bi_ref, lam_ref,
                wout_ref, y_ref, ext_ref, h_ref):
    rows = LRU_ROWS
    r = RNN_WIDTH

    @pl.when(pl.program_id(1) == 0)
    def _():
        ext_ref[...] = jnp.zeros_like(ext_ref)
        h_ref[...] = jnp.zeros_like(h_ref)

    gw = LRU_GROUP_W
    x = x_ref[0]
    h = _rmsnorm(x, g_ref[...]).astype(BF16)

    gr = _dot(h, win_ref[...])

    def recurrence(c):
        lanes = slice(c * gw, (c + 1) * gw)
        gate = _gelu_tanh(gr[:, c * gw:(c + 1) * gw])
        rec = gr[:, r + c * gw:r + (c + 1) * gw]
        ext = jnp.concatenate([ext_ref[:, lanes], rec], axis=0)
        ext_ref[:, lanes] = rec[rows - SUBLANES:, :]
        xc = cb_ref[:, lanes]
        for j in range(CONV_WIDTH):
            lag = CONV_WIDTH - 1 - j
            past = pltpu.roll(ext, lag, 0)[SUBLANES:, :] if lag else rec
            xc = xc + past * cw_ref[j:j + 1, lanes]
        both = _dot(xc.astype(BF16), wg_ref[c])
        rg = _sigmoid(both[:, :gw] + ba_ref[:, lanes])
        ig = _sigmoid(both[:, gw:] + bi_ref[:, lanes])
        log_a = (-RG_C * _softplus(-lam_ref[:, lanes])) * rg
        a = jnp.exp(log_a)
        th = jnp.tanh(log_a)
        w = -2.0 * th / (1.0 - th)
        bb = jnp.where(w > 0.0, w * lax.rsqrt(w), 0.0) * (ig * xc)
        hs = _linear_scan_rows(a, bb, h_ref[0:1, lanes])
        h_ref[:, lanes] = jnp.broadcast_to(hs[rows - 1:rows, :], (SUBLANES, gw))
        return (hs * gate).astype(BF16)

    y = jnp.concatenate([recurrence(c) for c in range(LRU_GROUPS)], axis=1)
    y_ref[0] = x + _dot(y, wout_ref[...])


def _block_diag_groups(w):
    per = RNN_BLOCKS // LRU_GROUPS
    w = w.reshape(LRU_GROUPS, per, RNN_BLOCK_W, 1, RNN_BLOCK_W)
    on_diag = jnp.eye(per, dtype=bool).reshape(1, per, 1, per, 1)
    return jnp.where(on_diag, w, 0.0).reshape(LRU_GROUPS, LRU_GROUP_W, LRU_GROUP_W)


def _lru(x, g, w_in, conv_w, conv_b, w_a, b_a, w_i, b_i, lam, w_out):
    b, t, _ = x.shape
    r = RNN_WIDTH
    w_gates = jnp.concatenate([_block_diag_groups(w_a), _block_diag_groups(w_i)],
                              axis=2).astype(BF16)
    row = pl.BlockSpec((1, LRU_ROWS, D_MODEL), lambda bi, ti: (bi, ti, 0))
    vec = _resident((1, r))
    return pl.pallas_call(
        _lru_kernel,
        grid=(b, t // LRU_ROWS),
        in_specs=[row, _resident((1, D_MODEL)), _resident((D_MODEL, 2 * r)),
                  _resident((CONV_WIDTH, r)), vec,
                  _resident((LRU_GROUPS, LRU_GROUP_W, 2 * LRU_GROUP_W)), vec, vec, vec,
                  _resident((r, D_MODEL))],
        out_specs=row,
        out_shape=jax.ShapeDtypeStruct((b, t, D_MODEL), F32),
        scratch_shapes=[pltpu.VMEM((SUBLANES, r), F32),
                        pltpu.VMEM((SUBLANES, r), F32)],
        compiler_params=_params("arbitrary", "arbitrary"),
        name="rglru",
    )(x, g.reshape(1, D_MODEL), w_in.astype(BF16), conv_w, conv_b.reshape(1, r), w_gates,
      b_a.reshape(1, r), b_i.reshape(1, r), lam.reshape(1, r), w_out.astype(BF16))


def kernel(x, ffn1_norm, ffn1_w_gu, ffn1_w_down, mix_norm, ffn2_norm, ffn2_w_gu, ffn2_w_down,
           fox_w_in, fox_b_f, fox_w_out, lru_w_in, lru_conv_w, lru_conv_b, lru_w_a, lru_b_a,
           lru_w_i, lru_b_i, lru_lambda, lru_w_out, final_norm):
    b, t, d = x.shape
    n = b * t
    for i in range(DEPTH):
        x = _ffn(x.reshape(n, d), ffn1_norm[i], ffn1_w_gu, ffn1_w_down, i, final_norm,
                 False).reshape(b, t, d)
        j = i // N_MIXERS
        mixer = None
        if i % N_MIXERS == 0:
            ka, qt, vt = _fox_proj(x, mix_norm[i], fox_w_in[j], fox_b_f[j])
            mixer = (_fox_attn(ka, qt, vt).reshape(n, d), fox_w_out[j])
        else:
            x = _lru(x, mix_norm[i], lru_w_in[j], lru_conv_w[j], lru_conv_b[j], lru_w_a[j],
                     lru_b_a[j], lru_w_i[j], lru_b_i[j], lru_lambda[j], lru_w_out[j])
        x = _ffn(x.reshape(n, d), ffn2_norm[i], ffn2_w_gu, ffn2_w_down, i, final_norm,
                 i == DEPTH - 1, mixer).reshape(b, t, d)
    return x
```

```python
import functools

import jax
import jax.numpy as jnp
import numpy as np
from jax import lax
from jax.experimental import pallas as pl
from jax.experimental.pallas import tpu as pltpu

D_MODEL = 1024
DEPTH = 2
N_MIXERS = 2
FOX_HEADS = 16
FOX_HEAD_DIM = D_MODEL // FOX_HEADS
RNN_WIDTH = 1280
RNN_BLOCKS = 16
RNN_BLOCK_W = RNN_WIDTH // RNN_BLOCKS
CONV_WIDTH = 4
RG_C = 8.0
D_FF = 2816
RMS_EPS = 1e-6
LOG2E = float(np.log2(np.e))

LANES = 128
SUBLANES = 8
BF16_SUBLANES = 16
VMEM_LIMIT_BYTES = 56 * 1024 * 1024

FFN_ROWS = 1024
FFN_SUB_ROWS = 512
FFN_STAGE_STEPS = 16
MXU_TILE = 256
FFN_CHUNK_BOUNDS = (0, 6 * MXU_TILE, D_FF)
PROJ_ROWS = 512
ATTN_BLOCK = 512
ATTN_KEY_BLOCKS = 2
ATTN_UNIT_KEYS = 256
ATTN_LOOKAHEAD = 2
LRU_ROWS = 512
LRU_SUB_ROWS = 256
LRU_GROUPS = 2
LRU_GROUP_W = RNN_WIDTH // LRU_GROUPS

HEAD_K = LANES
AUG_SLOTS = SUBLANES
F_PIECES = 3
V_ROWS = FOX_HEAD_DIM + BF16_SUBLANES

BF16 = jnp.bfloat16
F32 = jnp.float32


def _params(*sem):
    return pltpu.CompilerParams(dimension_semantics=sem, vmem_limit_bytes=VMEM_LIMIT_BYTES)


def _resident(shape):
    nd = len(shape)
    return pl.BlockSpec(shape, lambda *_: (0,) * nd, pipeline_mode=pl.Buffered(1))


def _rmsnorm(x, g):
    ms = jnp.mean(x * x, axis=-1, keepdims=True)
    return x * lax.rsqrt(ms + RMS_EPS) * g


def _dot(a, b):
    return jnp.dot(a, b, preferred_element_type=F32)


def _dot_nt(a, b):
    return lax.dot_general(a, b, (((1,), (1,)), ((), ())), preferred_element_type=F32)


def _sigmoid(x):
    return 1.0 / (1.0 + jnp.exp2(x * (-LOG2E)))


def _ffn_kernel(*refs, final_norm, mixer_proj):
    if mixer_proj:
        x_ref, m_ref, mw_ref, g_ref, wgu32_ref, wd32_ref, fg_ref, o_ref, wgu_ref, wd_ref = refs
    else:
        x_ref, g_ref, wgu32_ref, wd32_ref, fg_ref, o_ref, wgu_ref, wd_ref = refs
        m_ref = mw_ref = None
    step = pl.program_id(0)

    @pl.when(step < FFN_STAGE_STEPS)
    def _():
        gu_rows, d_rows = wgu32_ref.shape[0], wd32_ref.shape[0]
        wgu_ref[pl.ds(pl.multiple_of(step * gu_rows, gu_rows), gu_rows), :] = (
            wgu32_ref[...].astype(BF16))
        wd_ref[pl.ds(pl.multiple_of(step * d_rows, d_rows), d_rows), :] = (
            wd32_ref[...].astype(BF16))

    @pl.when(step >= FFN_STAGE_STEPS)
    def _():
        _ffn_tile(x_ref, m_ref, mw_ref, g_ref, wgu_ref, wd_ref, fg_ref, o_ref, final_norm)


def _ffn_tile(x_ref, m_ref, mw_ref, g_ref, wgu_ref, wd_ref, fg_ref, o_ref, final_norm):
    chunks = list(zip(FFN_CHUNK_BOUNDS[:-1], FFN_CHUNK_BOUNDS[1:]))
    n_sub = FFN_ROWS // FFN_SUB_ROWS

    def rows(i):
        return slice(i * FFN_SUB_ROWS, (i + 1) * FFN_SUB_ROWS)

    def normed(i):
        x = x_ref[rows(i), :]
        if m_ref is not None:
            x = x + _dot(m_ref[rows(i), :], mw_ref[...])
            o_ref[rows(i), :] = x
        return _rmsnorm(x, g_ref[...]).astype(BF16)

    def finish(i, acc):
        base = x_ref if m_ref is None else o_ref
        y = base[rows(i), :] + 0.5 * acc
        if final_norm:
            y = _rmsnorm(y, fg_ref[...])
        o_ref[rows(i), :] = y

    h = normed(0)
    done = None
    for i in range(n_sub):
        acc = None
        for c, (lo, hi) in enumerate(chunks):
            gate = _dot(h, wgu_ref[:, lo:hi])
            if c == 0:
                h_next = normed(i + 1) if i + 1 < n_sub else None
                if done is not None:
                    finish(*done)
            up = _dot(h, wgu_ref[:, D_FF + lo:D_FF + hi])
            a = (gate * _sigmoid(gate) * up).astype(BF16)
            part = _dot(a, wd_ref[lo:hi, :])
            acc = part if acc is None else acc + part
        done = (i, acc)
        h = h_next
    finish(*done)


def _ffn(x2d, g, w_gu_all, w_down_all, layer, final_g, final_norm, mixer=None):
    n = x2d.shape[0]
    stage = FFN_STAGE_STEPS
    last = stage - 1

    def row_map(i):
        return (jnp.maximum(i - stage, 0), 0)

    row = pl.BlockSpec((FFN_ROWS, D_MODEL), row_map)
    operands, specs = [x2d], [row]
    if mixer is not None:
        m2d, w_m = mixer
        operands += [m2d, w_m.astype(BF16)]
        specs += [pl.BlockSpec((FFN_ROWS, m2d.shape[1]), row_map), _resident(w_m.shape)]
    operands += [g.reshape(1, D_MODEL), w_gu_all, w_down_all, final_g.reshape(1, D_MODEL)]
    specs += [_resident((1, D_MODEL)),
              pl.BlockSpec((None, D_MODEL // stage, 2 * D_FF),
                           lambda i: (layer, jnp.minimum(i, last), 0)),
              pl.BlockSpec((None, D_FF // stage, D_MODEL),
                           lambda i: (layer, jnp.minimum(i, last), 0)),
              _resident((1, D_MODEL))]
    return pl.pallas_call(
        functools.partial(_ffn_kernel, final_norm=final_norm, mixer_proj=mixer is not None),
        grid=(stage + n // FFN_ROWS,),
        in_specs=specs,
        out_specs=row,
        out_shape=jax.ShapeDtypeStruct((n, D_MODEL), F32),
        scratch_shapes=[pltpu.VMEM((D_MODEL, 2 * D_FF), BF16),
                        pltpu.VMEM((D_FF, D_MODEL), BF16)],
        compiler_params=_params("arbitrary"),
        name="ffn",
    )(*operands)


def _log_sigmoid(z):
    return jnp.minimum(z, 0.0) - jnp.log1p(jnp.exp(-jnp.abs(z)))


def _cumsum_lanes(v):
    n = v.shape[-1]
    lane = lax.broadcasted_iota(jnp.int32, v.shape, v.ndim - 1)
    s = 1
    while s < n:
        v = v + jnp.where(lane >= s, pltpu.roll(v, s, v.ndim - 1), 0.0)
        s *= 2
    return v


def _split_bf16_pieces(f):
    p1 = f.astype(BF16).astype(F32)
    r1 = f - p1
    p2 = r1.astype(BF16).astype(F32)
    return p1, p2, r1 - p2


def _fox_proj_kernel(x_ref, g_ref, wqkv_ref, place_ref, wft_ref, bf_ref,
                     ka_ref, qt_ref, vt_ref, carry_ref):
    rows = PROJ_ROWS
    dh = FOX_HEAD_DIM

    @pl.when(pl.program_id(1) == 0)
    def _():
        carry_ref[...] = jnp.zeros_like(carry_ref)

    h = _rmsnorm(x_ref[0], g_ref[...]).astype(BF16)
    log_f = _log_sigmoid(_dot_nt(wft_ref[...], h) + bf_ref[...])
    cum = _cumsum_lanes(log_f) + carry_ref[:, 0:1]
    carry_ref[...] = jnp.broadcast_to(cum[:, rows - 1:rows], carry_ref.shape)
    pieces = _split_bf16_pieces(cum * LOG2E)

    qkv = _dot(h, wqkv_ref[...])
    sub_q = lax.broadcasted_iota(jnp.int32, (BF16_SUBLANES, rows), 0)
    sub_k = lax.broadcasted_iota(jnp.int32, (AUG_SLOTS, rows), 0)
    vt_tail = jnp.where(sub_q == 0, 1.0, 0.0).astype(BF16)
    q_scale = LOG2E * dh ** -0.5
    ka_aug = []
    for hh in range(FOX_HEADS):
        aug_q = jnp.where(sub_q < 2 * F_PIECES, 1.0, 0.0)
        aug_k = jnp.where(sub_k < F_PIECES, 1.0, 0.0)
        for j in range(F_PIECES):
            pj = pieces[j][hh:hh + 1]
            aug_q = jnp.where(sub_q == j, pj, aug_q)
            aug_k = jnp.where(sub_k == F_PIECES + j, -pj, aug_k)
        qt_ref[0, hh, dh:dh + BF16_SUBLANES, :] = aug_q.astype(BF16)
        qt_ref[0, hh, dh + BF16_SUBLANES:, :] = jnp.zeros(
            (HEAD_K - dh - BF16_SUBLANES, rows), BF16)
        vt_ref[0, hh, dh:, :] = vt_tail
        ka_aug.append(aug_k)
    ka_tok = jnp.concatenate(ka_aug, axis=0).T.astype(BF16)
    aug_wide = _dot(ka_tok, place_ref[...])
    lane = lax.broadcasted_iota(jnp.int32, (rows, LANES), 1)
    for pair in range(FOX_HEADS // 2):
        cols = slice(pair * LANES, (pair + 1) * LANES)
        q_t = (qkv[:, cols] * q_scale).T
        v_t = qkv[:, 2 * D_MODEL + pair * LANES:2 * D_MODEL + (pair + 1) * LANES].T
        k_even = qkv[:, D_MODEL + pair * LANES:D_MODEL + (pair + 1) * LANES]
        k_odd = pltpu.roll(k_even, dh, 1)
        for e, k_src in enumerate((k_even, k_odd)):
            hh = 2 * pair + e
            qt_ref[0, hh, 0:dh, :] = q_t[e * dh:(e + 1) * dh].astype(BF16)
            vt_ref[0, hh, 0:dh, :] = v_t[e * dh:(e + 1) * dh].astype(BF16)
            ka_ref[0, hh] = jnp.where(lane < dh, k_src,
                                      aug_wide[:, hh * HEAD_K:(hh + 1) * HEAD_K]).astype(BF16)


def _aug_placement():
    p = np.zeros((FOX_HEADS * AUG_SLOTS, FOX_HEADS * HEAD_K), np.float32)
    for hh in range(FOX_HEADS):
        for j in range(AUG_SLOTS):
            p[hh * AUG_SLOTS + j, hh * HEAD_K + FOX_HEAD_DIM + j] = 1.0
    return p


def _fox_block_spec(tile, index_map):
    return pl.BlockSpec((None, 1, FOX_HEADS) + tile, lambda *a: index_map(*a) + (0, 0, 0))


def _fox_proj(x, g, w_in, b_f):
    b, t, d = x.shape
    assert PROJ_ROWS == ATTN_BLOCK
    nb = t // PROJ_ROWS
    hk = FOX_HEADS * HEAD_K
    w_qkv = w_in[:, :3 * d].astype(BF16)
    w_f = w_in[:, 3 * d:]
    place = jnp.asarray(_aug_placement(), BF16)
    row = pl.BlockSpec((1, PROJ_ROWS, d), lambda bi, ti: (bi, ti, 0))
    return pl.pallas_call(
        _fox_proj_kernel,
        grid=(b, nb),
        in_specs=[row, _resident((1, d)), _resident((d, 3 * d)),
                  _resident((FOX_HEADS * AUG_SLOTS, hk)), _resident((FOX_HEADS, d)),
                  _resident((FOX_HEADS, 1))],
        out_specs=[_fox_block_spec((PROJ_ROWS, HEAD_K), lambda bi, ti: (bi, ti)),
                   _fox_block_spec((HEAD_K, PROJ_ROWS), lambda bi, ti: (bi, ti)),
                   _fox_block_spec((V_ROWS, PROJ_ROWS), lambda bi, ti: (bi, ti))],
        out_shape=[jax.ShapeDtypeStruct((b, nb, FOX_HEADS, PROJ_ROWS, HEAD_K), BF16),
                   jax.ShapeDtypeStruct((b, nb, FOX_HEADS, HEAD_K, PROJ_ROWS), BF16),
                   jax.ShapeDtypeStruct((b, nb, FOX_HEADS, V_ROWS, PROJ_ROWS), BF16)],
        scratch_shapes=[pltpu.VMEM((FOX_HEADS, LANES), F32)],
        compiler_params=_params("arbitrary", "arbitrary"),
        name="fox_proj",
    )(x, g.reshape(1, d), w_qkv, place, w_f.T.astype(BF16), b_f.reshape(FOX_HEADS, 1))


def _fox_attn_kernel(qi_ref, kj_ref, ka_ref, qt_ref, vt_ref, o_ref, m_ref, acc_ref):
    step = pl.program_id(1)
    qi = qi_ref[step]
    first = kj_ref[step] * ATTN_KEY_BLOCKS
    blk = ATTN_BLOCK
    dh = FOX_HEAD_DIM

    @pl.when(first == 0)
    def _():
        m_ref[...] = jnp.full_like(m_ref, -jnp.inf)
        acc_ref[...] = jnp.zeros_like(acc_ref)

    def run(n_blocks, diag_last):
        uk = ATTN_UNIT_KEYS
        parts = blk // uk
        if diag_last:
            key = lax.broadcasted_iota(jnp.int32, (uk, blk), 0)
            qry = lax.broadcasted_iota(jnp.int32, (uk, blk), 1)

        def split(u):
            jp, h = divmod(u, FOX_HEADS)
            j, part = divmod(jp, parts)
            return j, part, h

        def scores(u):
            j, part, h = split(u)
            s = _dot(ka_ref[j, h, part * uk:(part + 1) * uk, :], qt_ref[0, h])
            if diag_last and j == n_blocks - 1:
                s = jnp.where(key + part * uk <= qry, s, -jnp.inf)
            m_prev = m_ref[h]
            m_new = jnp.maximum(m_prev, jnp.max(s, axis=0, keepdims=True))
            m_ref[h] = m_new
            return s, m_new, jnp.exp2(m_prev - m_new)

        units = n_blocks * parts * FOX_HEADS
        ahead = [scores(u) for u in range(ATTN_LOOKAHEAD)]
        for u in range(units):
            j, part, h = split(u)
            s, m_new, alpha = ahead.pop(0)
            if u + ATTN_LOOKAHEAD < units:
                ahead.append(scores(u + ATTN_LOOKAHEAD))
            p = jnp.exp2(s - m_new).astype(BF16)
            pv = _dot(vt_ref[j, h, :, part * uk:(part + 1) * uk], p)
            acc_ref[h] = alpha * acc_ref[h] + pv

    def finalize():
        for hp in range(FOX_HEADS // 2):
            pair = []
            for h in (2 * hp, 2 * hp + 1):
                a = acc_ref[h]
                pair.append(a[0:dh] * (1.0 / a[dh:dh + 1]))
            o_ref[0, :, hp * LANES:(hp + 1) * LANES] = (
                jnp.concatenate(pair, axis=0).T.astype(BF16))

    @pl.when(first + ATTN_KEY_BLOCKS - 1 < qi)
    def _():
        run(ATTN_KEY_BLOCKS, False)

    for n in range(1, ATTN_KEY_BLOCKS + 1):
        @pl.when(first + n - 1 == qi)
        def _(n=n):
            run(n, True)
            finalize()


def _fox_attn(ka, qt, vt):
    b, nb = ka.shape[:2]
    t = nb * ATTN_BLOCK
    assert nb % ATTN_KEY_BLOCKS == 0
    groups = [-(-(i + 1) // ATTN_KEY_BLOCKS) for i in range(nb)]
    qi_tab = np.concatenate([np.full(g, i) for i, g in enumerate(groups)]).astype(np.int32)
    kj_tab = np.concatenate([np.arange(g) for g in groups]).astype(np.int32)
    o_spec = pl.BlockSpec((1, ATTN_BLOCK, D_MODEL), lambda bi, s, qi, kj: (bi, qi[s], 0))

    def key_spec(tile):
        return pl.BlockSpec((None, ATTN_KEY_BLOCKS, FOX_HEADS) + tile,
                            lambda bi, s, qi, kj: (bi, kj[s], 0, 0, 0))

    grid_spec = pltpu.PrefetchScalarGridSpec(
        num_scalar_prefetch=2,
        grid=(b, len(qi_tab)),
        in_specs=[key_spec((ATTN_BLOCK, HEAD_K)),
                  _fox_block_spec((HEAD_K, ATTN_BLOCK), lambda bi, s, qi, kj: (bi, qi[s])),
                  key_spec((V_ROWS, ATTN_BLOCK))],
        out_specs=o_spec,
        scratch_shapes=[pltpu.VMEM((FOX_HEADS, 1, ATTN_BLOCK), F32),
                        pltpu.VMEM((FOX_HEADS, V_ROWS, ATTN_BLOCK), F32)],
    )
    return pl.pallas_call(
        _fox_attn_kernel,
        grid_spec=grid_spec,
        out_shape=jax.ShapeDtypeStruct((b, t, D_MODEL), BF16),
        compiler_params=_params("arbitrary", "arbitrary"),
        name="fox_attn",
    )(jnp.asarray(qi_tab), jnp.asarray(kj_tab), ka, qt, vt)


def _gelu_tanh(x):
    k1 = float(2.0 * np.sqrt(2.0 / np.pi))
    return x * _sigmoid(x * (k1 + (k1 * 0.044715) * (x * x)))


def _softplus(z):
    return jnp.maximum(z, 0.0) + jnp.log1p(jnp.exp(-jnp.abs(z)))


def _linear_scan_rows(a, b, h0):
    rows, n = a.shape
    groups = rows // SUBLANES
    a = a.reshape(groups, SUBLANES, n)
    b = b.reshape(groups, SUBLANES, n)
    sub = lax.broadcasted_iota(jnp.int32, a.shape, 1)
    s = 1
    while s < SUBLANES:
        keep = sub >= s
        b = jnp.where(keep, a * pltpu.roll(b, s, 1) + b, b)
        a = jnp.where(keep, a * pltpu.roll(a, s, 1), a)
        s *= 2
    out = []
    carry = h0
    for g in range(groups):
        hg = b[g] + a[g] * carry
        out.append(hg)
        carry = hg[SUBLANES - 1:SUBLANES, :]
    return jnp.concatenate(out, axis=0)


def _lru_kernel(x_ref, g_ref, win_ref, cw_ref, cb_ref, wg_ref, ba_ref, bi_ref, lam_ref,
                wout_ref, y_ref, ext_ref, h_ref):
    rows = LRU_ROWS
    r = RNN_WIDTH

    @pl.when(pl.program_id(1) == 0)
    def _():
        ext_ref[...] = jnp.zeros_like(ext_ref)
        h_ref[...] = jnp.zeros_like(h_ref)

    gw = LRU_GROUP_W
    sub = LRU_SUB_ROWS
    n_sub = rows // sub
    groups = [slice(c * gw, (c + 1) * gw) for c in range(LRU_GROUPS)]

    def tile(i):
        return slice(i * sub, (i + 1) * sub)

    def normed(i):
        return _rmsnorm(x_ref[0, tile(i), :], g_ref[...]).astype(BF16)

    def conv(rec_all):
        out = []
        for lanes in groups:
            rec = rec_all[:, lanes]
            ext = jnp.concatenate([ext_ref[:, lanes], rec], axis=0)
            ext_ref[:, lanes] = rec[sub - SUBLANES:, :]
            xc = cb_ref[:, lanes]
            for j in range(CONV_WIDTH):
                lag = CONV_WIDTH - 1 - j
                past = pltpu.roll(ext, lag, 0)[SUBLANES:, :] if lag else rec
                xc = xc + past * cw_ref[j:j + 1, lanes]
            out.append(xc)
        return out

    def scan(xcs, boths, gate_all):
        ys = []
        for lanes, xc, both in zip(groups, xcs, boths):
            rg = _sigmoid(both[:, :gw] + ba_ref[:, lanes])
            ig = _sigmoid(both[:, gw:] + bi_ref[:, lanes])
            log_a = (-RG_C * _softplus(-lam_ref[:, lanes])) * rg
            a = jnp.exp(log_a)
            th = jnp.tanh(log_a)
            w = -2.0 * th / (1.0 - th)
            bb = jnp.where(w > 0.0, w * lax.rsqrt(w), 0.0) * (ig * xc)
            hs = _linear_scan_rows(a, bb, h_ref[0:1, lanes])
            h_ref[:, lanes] = jnp.broadcast_to(hs[sub - 1:sub, :], (SUBLANES, gw))
            ys.append((hs * _gelu_tanh(gate_all[:, lanes])).astype(BF16))
        return jnp.concatenate(ys, axis=1)

    h = normed(0)
    rec_all = _dot(h, win_ref[:, r:])
    gate_all = _dot(h, win_ref[:, :r])
    for i in range(n_sub):
        more = i + 1 < n_sub
        if more:
            h = normed(i + 1)
            rec_next = _dot(h, win_ref[:, r:])
        xcs = conv(rec_all)
        boths = [_dot(xc.astype(BF16), wg_ref[c]) for c, xc in enumerate(xcs)]
        if more:
            gate_next = _dot(h, win_ref[:, :r])
        y = scan(xcs, boths, gate_all)
        y_ref[0, tile(i), :] = x_ref[0, tile(i), :] + _dot(y, wout_ref[...])
        if more:
            rec_all, gate_all = rec_next, gate_next


def _block_diag_groups(w):
    per = RNN_BLOCKS // LRU_GROUPS
    w = w.reshape(LRU_GROUPS, per, RNN_BLOCK_W, 1, RNN_BLOCK_W)
    on_diag = jnp.eye(per, dtype=bool).reshape(1, per, 1, per, 1)
    return jnp.where(on_diag, w, 0.0).reshape(LRU_GROUPS, LRU_GROUP_W, LRU_GROUP_W)


def _lru(x, g, w_in, conv_w, conv_b, w_a, b_a, w_i, b_i, lam, w_out):
    b, t, _ = x.shape
    r = RNN_WIDTH
    w_gates = jnp.concatenate([_block_diag_groups(w_a), _block_diag_groups(w_i)],
                              axis=2).astype(BF16)
    row = pl.BlockSpec((1, LRU_ROWS, D_MODEL), lambda bi, ti: (bi, ti, 0))
    vec = _resident((1, r))
    return pl.pallas_call(
        _lru_kernel,
        grid=(b, t // LRU_ROWS),
        in_specs=[row, _resident((1, D_MODEL)), _resident((D_MODEL, 2 * r)),
                  _resident((CONV_WIDTH, r)), vec,
                  _resident((LRU_GROUPS, LRU_GROUP_W, 2 * LRU_GROUP_W)), vec, vec, vec,
                  _resident((r, D_MODEL))],
        out_specs=row,
        out_shape=jax.ShapeDtypeStruct((b, t, D_MODEL), F32),
        scratch_shapes=[pltpu.VMEM((SUBLANES, r), F32),
                        pltpu.VMEM((SUBLANES, r), F32)],
        compiler_params=_params("arbitrary", "arbitrary"),
        name="rglru",
    )(x, g.reshape(1, D_MODEL), w_in.astype(BF16), conv_w, conv_b.reshape(1, r), w_gates,
      b_a.reshape(1, r), b_i.reshape(1, r), lam.reshape(1, r), w_out.astype(BF16))


def kernel(x, ffn1_norm, ffn1_w_gu, ffn1_w_down, mix_norm, ffn2_norm, ffn2_w_gu, ffn2_w_down,
           fox_w_in, fox_b_f, fox_w_out, lru_w_in, lru_conv_w, lru_conv_b, lru_w_a, lru_b_a,
           lru_w_i, lru_b_i, lru_lambda, lru_w_out, final_norm):
    b, t, d = x.shape
    n = b * t
    for i in range(DEPTH):
        x = _ffn(x.reshape(n, d), ffn1_norm[i], ffn1_w_gu, ffn1_w_down, i, final_norm,
                 False).reshape(b, t, d)
        j = i // N_MIXERS
        mixer = None
        if i % N_MIXERS == 0:
            ka, qt, vt = _fox_proj(x, mix_norm[i], fox_w_in[j], fox_b_f[j])
            mixer = (_fox_attn(ka, qt, vt).reshape(n, d), fox_w_out[j])
        else:
            x = _lru(x, mix_norm[i], lru_w_in[j], lru_conv_w[j], lru_conv_b[j], lru_w_a[j],
                     lru_b_a[j], lru_w_i[j], lru_b_i[j], lru_lambda[j], lru_w_out[j])
        x = _ffn(x.reshape(n, d), ffn2_norm[i], ffn2_w_gu, ffn2_w_down, i, final_norm,
                 i == DEPTH - 1, mixer).reshape(b, t, d)
    return x
```

```python
import functools

import jax
import jax.numpy as jnp
import numpy as np
from jax import lax
from jax.experimental import pallas as pl
from jax.experimental.pallas import tpu as pltpu

D_MODEL = 1024
DEPTH = 2
N_MIXERS = 2
FOX_HEADS = 16
FOX_HEAD_DIM = D_MODEL // FOX_HEADS
RNN_WIDTH = 1280
RNN_BLOCKS = 16
RNN_BLOCK_W = RNN_WIDTH // RNN_BLOCKS
CONV_WIDTH = 4
RG_C = 8.0
D_FF = 2816
RMS_EPS = 1e-6
LOG2E = float(np.log2(np.e))

LANES = 128
SUBLANES = 8
BF16_SUBLANES = 16
VMEM_LIMIT_BYTES = 56 * 1024 * 1024

FFN_ROWS = 1024
FFN_SUB_ROWS = 512
FFN_STAGE_STEPS = 16
MXU_TILE = 256
FFN_CHUNK_BOUNDS = (0, 6 * MXU_TILE, D_FF)
PROJ_ROWS = 512
ATTN_BLOCK = 512
ATTN_KEY_BLOCKS = 2
ATTN_UNIT_KEYS = 256
ATTN_LOOKAHEAD = 2
LRU_ROWS = 512
LRU_SUB_ROWS = 256
LRU_GROUPS = 2
LRU_GROUP_W = RNN_WIDTH // LRU_GROUPS

HEAD_K = LANES
AUG_SLOTS = SUBLANES
F_PIECES = 3
V_ROWS = FOX_HEAD_DIM + BF16_SUBLANES

BF16 = jnp.bfloat16
F32 = jnp.float32


def _params(*sem):
    return pltpu.CompilerParams(dimension_semantics=sem, vmem_limit_bytes=VMEM_LIMIT_BYTES)


def _resident(shape):
    nd = len(shape)
    return pl.BlockSpec(shape, lambda *_: (0,) * nd, pipeline_mode=pl.Buffered(1))


def _rmsnorm(x, g):
    ms = jnp.mean(x * x, axis=-1, keepdims=True)
    return x * lax.rsqrt(ms + RMS_EPS) * g


def _dot(a, b):
    return jnp.dot(a, b, preferred_element_type=F32)


def _dot_nt(a, b):
    return lax.dot_general(a, b, (((1,), (1,)), ((), ())), preferred_element_type=F32)


def _sigmoid(x):
    return 1.0 / (1.0 + jnp.exp2(x * (-LOG2E)))


def _ffn_kernel(*refs, final_norm, mixer_proj):
    if mixer_proj:
        x_ref, m_ref, mw_ref, g_ref, wgu32_ref, wd32_ref, fg_ref, o_ref, wgu_ref, wd_ref = refs
    else:
        x_ref, g_ref, wgu32_ref, wd32_ref, fg_ref, o_ref, wgu_ref, wd_ref = refs
        m_ref = mw_ref = None
    step = pl.program_id(0)

    @pl.when(step < FFN_STAGE_STEPS)
    def _():
        gu_rows, d_rows = wgu32_ref.shape[0], wd32_ref.shape[0]
        wgu_ref[pl.ds(pl.multiple_of(step * gu_rows, gu_rows), gu_rows), :] = (
            wgu32_ref[...].astype(BF16))
        wd_ref[pl.ds(pl.multiple_of(step * d_rows, d_rows), d_rows), :] = (
            wd32_ref[...].astype(BF16))

    @pl.when(step >= FFN_STAGE_STEPS)
    def _():
        _ffn_tile(x_ref, m_ref, mw_ref, g_ref, wgu_ref, wd_ref, fg_ref, o_ref, final_norm)


def _ffn_tile(x_ref, m_ref, mw_ref, g_ref, wgu_ref, wd_ref, fg_ref, o_ref, final_norm):
    chunks = list(zip(FFN_CHUNK_BOUNDS[:-1], FFN_CHUNK_BOUNDS[1:]))
    n_sub = FFN_ROWS // FFN_SUB_ROWS

    def rows(i):
        return slice(i * FFN_SUB_ROWS, (i + 1) * FFN_SUB_ROWS)

    def normed(i):
        x = x_ref[rows(i), :]
        if m_ref is not None:
            x = x + _dot(m_ref[rows(i), :], mw_ref[...])
            o_ref[rows(i), :] = x
        return _rmsnorm(x, g_ref[...]).astype(BF16)

    def finish(i, acc):
        base = x_ref if m_ref is None else o_ref
        y = base[rows(i), :] + 0.5 * acc
        if final_norm:
            y = _rmsnorm(y, fg_ref[...])
        o_ref[rows(i), :] = y

    h = normed(0)
    done = None
    for i in range(n_sub):
        acc = None
        for c, (lo, hi) in enumerate(chunks):
            gate = _dot(h, wgu_ref[:, lo:hi])
            if c == 0:
                h_next = normed(i + 1) if i + 1 < n_sub else None
                if done is not None:
                    finish(*done)
            up = _dot(h, wgu_ref[:, D_FF + lo:D_FF + hi])
            a = (gate * _sigmoid(gate) * up).astype(BF16)
            part = _dot(a, wd_ref[lo:hi, :])
            acc = part if acc is None else acc + part
        done = (i, acc)
        h = h_next
    finish(*done)


def _ffn(x2d, g, w_gu_all, w_down_all, layer, final_g, final_norm, mixer=None):
    n = x2d.shape[0]
    stage = FFN_STAGE_STEPS
    last = stage - 1

    def row_map(i):
        return (jnp.maximum(i - stage, 0), 0)

    row = pl.BlockSpec((FFN_ROWS, D_MODEL), row_map)
    operands, specs = [x2d], [row]
    if mixer is not None:
        m2d, w_m = mixer
        operands += [m2d, w_m.astype(BF16)]
        specs += [pl.BlockSpec((FFN_ROWS, m2d.shape[1]), row_map), _resident(w_m.shape)]
    operands += [g.reshape(1, D_MODEL), w_gu_all, w_down_all, final_g.reshape(1, D_MODEL)]
    specs += [_resident((1, D_MODEL)),
              pl.BlockSpec((None, D_MODEL // stage, 2 * D_FF),
                           lambda i: (layer, jnp.minimum(i, last), 0)),
              pl.BlockSpec((None, D_FF // stage, D_MODEL),
                           lambda i: (layer, jnp.minimum(i, last), 0)),
              _resident((1, D_MODEL))]
    return pl.pallas_call(
        functools.partial(_ffn_kernel, final_norm=final_norm, mixer_proj=mixer is not None),
        grid=(stage + n // FFN_ROWS,),
        in_specs=specs,
        out_specs=row,
        out_shape=jax.ShapeDtypeStruct((n, D_MODEL), F32),
        scratch_shapes=[pltpu.VMEM((D_MODEL, 2 * D_FF), BF16),
                        pltpu.VMEM((D_FF, D_MODEL), BF16)],
        compiler_params=_params("arbitrary"),
        name="ffn",
    )(*operands)


def _log_sigmoid(z):
    return jnp.minimum(z, 0.0) - jnp.log1p(jnp.exp(-jnp.abs(z)))


def _cumsum_lanes(v):
    n = v.shape[-1]
    lane = lax.broadcasted_iota(jnp.int32, v.shape, v.ndim - 1)
    s = 1
    while s < n:
        v = v + jnp.where(lane >= s, pltpu.roll(v, s, v.ndim - 1), 0.0)
        s *= 2
    return v


def _split_bf16_pieces(f):
    p1 = f.astype(BF16).astype(F32)
    r1 = f - p1
    p2 = r1.astype(BF16).astype(F32)
    return p1, p2, r1 - p2


def _fox_proj_kernel(x_ref, g_ref, win32_ref, place_ref, wft_ref, bf_ref,
                     ka_ref, qt_ref, vt_ref, carry_ref, wqkv_ref):
    rows = PROJ_ROWS
    dh = FOX_HEAD_DIM

    @pl.when((pl.program_id(0) == 0) & (pl.program_id(1) == 0))
    def _():
        wqkv_ref[...] = win32_ref[:, :3 * D_MODEL].astype(BF16)

    @pl.when(pl.program_id(1) == 0)
    def _():
        carry_ref[...] = jnp.zeros_like(carry_ref)

    h = _rmsnorm(x_ref[0], g_ref[...]).astype(BF16)
    log_f = _log_sigmoid(_dot_nt(wft_ref[...], h) + bf_ref[...])
    cum = _cumsum_lanes(log_f) + carry_ref[:, 0:1]
    carry_ref[...] = jnp.broadcast_to(cum[:, rows - 1:rows], carry_ref.shape)
    pieces = _split_bf16_pieces(cum * LOG2E)

    qkv = _dot(h, wqkv_ref[...])
    sub_q = lax.broadcasted_iota(jnp.int32, (BF16_SUBLANES, rows), 0)
    sub_k = lax.broadcasted_iota(jnp.int32, (AUG_SLOTS, rows), 0)
    vt_tail = jnp.where(sub_q == 0, 1.0, 0.0).astype(BF16)
    q_scale = LOG2E * dh ** -0.5
    ka_aug = []
    for hh in range(FOX_HEADS):
        aug_q = jnp.where(sub_q < 2 * F_PIECES, 1.0, 0.0)
        aug_k = jnp.where(sub_k < F_PIECES, 1.0, 0.0)
        for j in range(F_PIECES):
            pj = pieces[j][hh:hh + 1]
            aug_q = jnp.where(sub_q == j, pj, aug_q)
            aug_k = jnp.where(sub_k == F_PIECES + j, -pj, aug_k)
        qt_ref[0, hh, dh:dh + BF16_SUBLANES, :] = aug_q.astype(BF16)
        qt_ref[0, hh, dh + BF16_SUBLANES:, :] = jnp.zeros(
            (HEAD_K - dh - BF16_SUBLANES, rows), BF16)
        vt_ref[0, hh, dh:, :] = vt_tail
        ka_aug.append(aug_k)
    ka_tok = jnp.concatenate(ka_aug, axis=0).T.astype(BF16)
    aug_wide = _dot(ka_tok, place_ref[...])
    lane = lax.broadcasted_iota(jnp.int32, (rows, LANES), 1)
    for pair in range(FOX_HEADS // 2):
        cols = slice(pair * LANES, (pair + 1) * LANES)
        q_t = (qkv[:, cols] * q_scale).T
        v_t = qkv[:, 2 * D_MODEL + pair * LANES:2 * D_MODEL + (pair + 1) * LANES].T
        k_even = qkv[:, D_MODEL + pair * LANES:D_MODEL + (pair + 1) * LANES]
        k_odd = pltpu.roll(k_even, dh, 1)
        for e, k_src in enumerate((k_even, k_odd)):
            hh = 2 * pair + e
            qt_ref[0, hh, 0:dh, :] = q_t[e * dh:(e + 1) * dh].astype(BF16)
            vt_ref[0, hh, 0:dh, :] = v_t[e * dh:(e + 1) * dh].astype(BF16)
            ka_ref[0, hh] = jnp.where(lane < dh, k_src,
                                      aug_wide[:, hh * HEAD_K:(hh + 1) * HEAD_K]).astype(BF16)


def _aug_placement():
    p = np.zeros((FOX_HEADS * AUG_SLOTS, FOX_HEADS * HEAD_K), np.float32)
    for hh in range(FOX_HEADS):
        for j in range(AUG_SLOTS):
            p[hh * AUG_SLOTS + j, hh * HEAD_K + FOX_HEAD_DIM + j] = 1.0
    return p


def _fox_block_spec(tile, index_map):
    return pl.BlockSpec((None, 1, FOX_HEADS) + tile, lambda *a: index_map(*a) + (0, 0, 0))


def _fox_proj(x, g, w_in, b_f):
    b, t, d = x.shape
    assert PROJ_ROWS == ATTN_BLOCK
    nb = t // PROJ_ROWS
    hk = FOX_HEADS * HEAD_K
    w_f = w_in[:, 3 * d:]
    place = jnp.asarray(_aug_placement(), BF16)
    row = pl.BlockSpec((1, PROJ_ROWS, d), lambda bi, ti: (bi, ti, 0))
    return pl.pallas_call(
        _fox_proj_kernel,
        grid=(b, nb),
        in_specs=[row, _resident((1, d)), _resident(w_in.shape),
                  _resident((FOX_HEADS * AUG_SLOTS, hk)), _resident((FOX_HEADS, d)),
                  _resident((FOX_HEADS, 1))],
        out_specs=[_fox_block_spec((PROJ_ROWS, HEAD_K), lambda bi, ti: (bi, ti)),
                   _fox_block_spec((HEAD_K, PROJ_ROWS), lambda bi, ti: (bi, ti)),
                   _fox_block_spec((V_ROWS, PROJ_ROWS), lambda bi, ti: (bi, ti))],
        out_shape=[jax.ShapeDtypeStruct((b, nb, FOX_HEADS, PROJ_ROWS, HEAD_K), BF16),
                   jax.ShapeDtypeStruct((b, nb, FOX_HEADS, HEAD_K, PROJ_ROWS), BF16),
                   jax.ShapeDtypeStruct((b, nb, FOX_HEADS, V_ROWS, PROJ_ROWS), BF16)],
        scratch_shapes=[pltpu.VMEM((FOX_HEADS, LANES), F32), pltpu.VMEM((d, 3 * d), BF16)],
        compiler_params=_params("arbitrary", "arbitrary"),
        name="fox_proj",
    )(x, g.reshape(1, d), w_in, place, w_f.T.astype(BF16), b_f.reshape(FOX_HEADS, 1))


def _fox_attn_kernel(qi_ref, kj_ref, ka_ref, qt_ref, vt_ref, o_ref, m_ref, acc_ref):
    step = pl.program_id(1)
    qi = qi_ref[step]
    first = kj_ref[step] * ATTN_KEY_BLOCKS
    blk = ATTN_BLOCK
    dh = FOX_HEAD_DIM

    @pl.when(first == 0)
    def _():
        m_ref[...] = jnp.full_like(m_ref, -jnp.inf)
        acc_ref[...] = jnp.zeros_like(acc_ref)

    def run(n_blocks, diag_last):
        uk = ATTN_UNIT_KEYS
        parts = blk // uk
        if diag_last:
            key = lax.broadcasted_iota(jnp.int32, (uk, blk), 0)
            qry = lax.broadcasted_iota(jnp.int32, (uk, blk), 1)

        def split(u):
            jp, h = divmod(u, FOX_HEADS)
            j, part = divmod(jp, parts)
            return j, part, h

        def scores(u):
            j, part, h = split(u)
            s = _dot(ka_ref[j, h, part * uk:(part + 1) * uk, :], qt_ref[0, h])
            if diag_last and j == n_blocks - 1:
                s = jnp.where(key + part * uk <= qry, s, -jnp.inf)
            m_prev = m_ref[h]
            m_new = jnp.maximum(m_prev, jnp.max(s, axis=0, keepdims=True))
            m_ref[h] = m_new
            return s, m_new, jnp.exp2(m_prev - m_new)

        units = n_blocks * parts * FOX_HEADS
        ahead = [scores(u) for u in range(ATTN_LOOKAHEAD)]
        for u in range(units):
            j, part, h = split(u)
            s, m_new, alpha = ahead.pop(0)
            if u + ATTN_LOOKAHEAD < units:
                ahead.append(scores(u + ATTN_LOOKAHEAD))
            p = jnp.exp2(s - m_new).astype(BF16)
            pv = _dot(vt_ref[j, h, :, part * uk:(part + 1) * uk], p)
            acc_ref[h] = alpha * acc_ref[h] + pv

    def finalize():
        for hp in range(FOX_HEADS // 2):
            pair = []
            for h in (2 * hp, 2 * hp + 1):
                a = acc_ref[h]
                pair.append(a[0:dh] * (1.0 / a[dh:dh + 1]))
            o_ref[0, :, hp * LANES:(hp + 1) * LANES] = (
                jnp.concatenate(pair, axis=0).T.astype(BF16))

    @pl.when(first + ATTN_KEY_BLOCKS - 1 < qi)
    def _():
        run(ATTN_KEY_BLOCKS, False)

    for n in range(1, ATTN_KEY_BLOCKS + 1):
        @pl.when(first + n - 1 == qi)
        def _(n=n):
            run(n, True)
            finalize()


def _fox_attn(ka, qt, vt):
    b, nb = ka.shape[:2]
    t = nb * ATTN_BLOCK
    assert nb % ATTN_KEY_BLOCKS == 0
    groups = [-(-(i + 1) // ATTN_KEY_BLOCKS) for i in range(nb)]
    qi_tab = np.concatenate([np.full(g, i) for i, g in enumerate(groups)]).astype(np.int32)
    kj_tab = np.concatenate([np.arange(g) for g in groups]).astype(np.int32)
    o_spec = pl.BlockSpec((1, ATTN_BLOCK, D_MODEL), lambda bi, s, qi, kj: (bi, qi[s], 0))

    def key_spec(tile):
        return pl.BlockSpec((None, ATTN_KEY_BLOCKS, FOX_HEADS) + tile,
                            lambda bi, s, qi, kj: (bi, kj[s], 0, 0, 0))

    grid_spec = pltpu.PrefetchScalarGridSpec(
        num_scalar_prefetch=2,
        grid=(b, len(qi_tab)),
        in_specs=[key_spec((ATTN_BLOCK, HEAD_K)),
                  _fox_block_spec((HEAD_K, ATTN_BLOCK), lambda bi, s, qi, kj: (bi, qi[s])),
                  key_spec((V_ROWS, ATTN_BLOCK))],
        out_specs=o_spec,
        scratch_shapes=[pltpu.VMEM((FOX_HEADS, 1, ATTN_BLOCK), F32),
                        pltpu.VMEM((FOX_HEADS, V_ROWS, ATTN_BLOCK), F32)],
    )
    return pl.pallas_call(
        _fox_attn_kernel,
        grid_spec=grid_spec,
        out_shape=jax.ShapeDtypeStruct((b, t, D_MODEL), BF16),
        compiler_params=_params("arbitrary", "arbitrary"),
        name="fox_attn",
    )(jnp.asarray(qi_tab), jnp.asarray(kj_tab), ka, qt, vt)


def _gelu_tanh(x):
    k1 = float(2.0 * np.sqrt(2.0 / np.pi))
    return x * _sigmoid(x * (k1 + (k1 * 0.044715) * (x * x)))


def _softplus(z):
    return jnp.maximum(z, 0.0) + jnp.log1p(jnp.exp(-jnp.abs(z)))


def _linear_scan_rows(a, b, h0):
    rows, n = a.shape
    groups = rows // SUBLANES
    a = a.reshape(groups, SUBLANES, n)
    b = b.reshape(groups, SUBLANES, n)
    sub = lax.broadcasted_iota(jnp.int32, a.shape, 1)
    s = 1
    while s < SUBLANES:
        keep = sub >= s
        b = jnp.where(keep, a * pltpu.roll(b, s, 1) + b, b)
        a = jnp.where(keep, a * pltpu.roll(a, s, 1), a)
        s *= 2
    out = []
    carry = h0
    for g in range(groups):
        hg = b[g] + a[g] * carry
        out.append(hg)
        carry = hg[SUBLANES - 1:SUBLANES, :]
    return jnp.concatenate(out, axis=0)


def _lru_kernel(x_ref, g_ref, win32_ref, cw_ref, cb_ref, wg_ref, ba_ref, bi_ref, lam_ref,
                wout32_ref, y_ref, ext_ref, h_ref, win_ref, wout_ref):
    rows = LRU_ROWS
    r = RNN_WIDTH

    @pl.when((pl.program_id(0) == 0) & (pl.program_id(1) == 0))
    def _():
        win_ref[...] = win32_ref[...].astype(BF16)
        wout_ref[...] = wout32_ref[...].astype(BF16)

    @pl.when(pl.program_id(1) == 0)
    def _():
        ext_ref[...] = jnp.zeros_like(ext_ref)
        h_ref[...] = jnp.zeros_like(h_ref)

    gw = LRU_GROUP_W
    sub = LRU_SUB_ROWS
    n_sub = rows // sub
    groups = [slice(c * gw, (c + 1) * gw) for c in range(LRU_GROUPS)]

    def tile(i):
        return slice(i * sub, (i + 1) * sub)

    def normed(i):
        return _rmsnorm(x_ref[0, tile(i), :], g_ref[...]).astype(BF16)

    def conv(rec_all):
        out = []
        for lanes in groups:
            rec = rec_all[:, lanes]
            ext = jnp.concatenate([ext_ref[:, lanes], rec], axis=0)
            ext_ref[:, lanes] = rec[sub - SUBLANES:, :]
            xc = cb_ref[:, lanes]
            for j in range(CONV_WIDTH):
                lag = CONV_WIDTH - 1 - j
                past = pltpu.roll(ext, lag, 0)[SUBLANES:, :] if lag else rec
                xc = xc + past * cw_ref[j:j + 1, lanes]
            out.append(xc)
        return out

    def scan(xcs, boths, gate_all):
        ys = []
        for lanes, xc, both in zip(groups, xcs, boths):
            rg = _sigmoid(both[:, :gw] + ba_ref[:, lanes])
            ig = _sigmoid(both[:, gw:] + bi_ref[:, lanes])
            log_a = (-RG_C * _softplus(-lam_ref[:, lanes])) * rg
            a = jnp.exp(log_a)
            th = jnp.tanh(log_a)
            w = -2.0 * th / (1.0 - th)
            bb = jnp.where(w > 0.0, w * lax.rsqrt(w), 0.0) * (ig * xc)
            hs = _linear_scan_rows(a, bb, h_ref[0:1, lanes])
            h_ref[:, lanes] = jnp.broadcast_to(hs[sub - 1:sub, :], (SUBLANES, gw))
            ys.append((hs * _gelu_tanh(gate_all[:, lanes])).astype(BF16))
        return jnp.concatenate(ys, axis=1)

    h = normed(0)
    rec_all = _dot(h, win_ref[:, r:])
    gate_all = _dot(h, win_ref[:, :r])
    for i in range(n_sub):
        more = i + 1 < n_sub
        if more:
            h = normed(i + 1)
            rec_next = _dot(h, win_ref[:, r:])
        xcs = conv(rec_all)
        boths = [_dot(xc.astype(BF16), wg_ref[c]) for c, xc in enumerate(xcs)]
        if more:
            gate_next = _dot(h, win_ref[:, :r])
        y = scan(xcs, boths, gate_all)
        y_ref[0, tile(i), :] = x_ref[0, tile(i), :] + _dot(y, wout_ref[...])
        if more:
            rec_all, gate_all = rec_next, gate_next


def _block_diag_groups(w):
    per = RNN_BLOCKS // LRU_GROUPS
    w = w.reshape(LRU_GROUPS, per, RNN_BLOCK_W, 1, RNN_BLOCK_W)
    on_diag = jnp.eye(per, dtype=bool).reshape(1, per, 1, per, 1)
    return jnp.where(on_diag, w, 0.0).reshape(LRU_GROUPS, LRU_GROUP_W, LRU_GROUP_W)


def _lru(x, g, w_in, conv_w, conv_b, w_a, b_a, w_i, b_i, lam, w_out):
    b, t, _ = x.shape
    r = RNN_WIDTH
    w_gates = jnp.concatenate([_block_diag_groups(w_a), _block_diag_groups(w_i)],
                              axis=2).astype(BF16)
    row = pl.BlockSpec((1, LRU_ROWS, D_MODEL), lambda bi, ti: (bi, ti, 0))
    vec = _resident((1, r))
    return pl.pallas_call(
        _lru_kernel,
        grid=(b, t // LRU_ROWS),
        in_specs=[row, _resident((1, D_MODEL)), _resident((D_MODEL, 2 * r)),
                  _resident((CONV_WIDTH, r)), vec,
                  _resident((LRU_GROUPS, LRU_GROUP_W, 2 * LRU_GROUP_W)), vec, vec, vec,
                  _resident((r, D_MODEL))],
        out_specs=row,
        out_shape=jax.ShapeDtypeStruct((b, t, D_MODEL), F32),
        scratch_shapes=[pltpu.VMEM((SUBLANES, r), F32),
                        pltpu.VMEM((SUBLANES, r), F32),
                        pltpu.VMEM((D_MODEL, 2 * r), BF16),
                        pltpu.VMEM((r, D_MODEL), BF16)],
        compiler_params=_params("arbitrary", "arbitrary"),
        name="rglru",
    )(x, g.reshape(1, D_MODEL), w_in, conv_w, conv_b.reshape(1, r), w_gates,
      b_a.reshape(1, r), b_i.reshape(1, r), lam.reshape(1, r), w_out)


def kernel(x, ffn1_norm, ffn1_w_gu, ffn1_w_down, mix_norm, ffn2_norm, ffn2_w_gu, ffn2_w_down,
           fox_w_in, fox_b_f, fox_w_out, lru_w_in, lru_conv_w, lru_conv_b, lru_w_a, lru_b_a,
           lru_w_i, lru_b_i, lru_lambda, lru_w_out, final_norm):
    b, t, d = x.shape
    n = b * t
    for i in range(DEPTH):
        x = _ffn(x.reshape(n, d), ffn1_norm[i], ffn1_w_gu, ffn1_w_down, i, final_norm,
                 False).reshape(b, t, d)
        j = i // N_MIXERS
        mixer = None
        if i % N_MIXERS == 0:
            ka, qt, vt = _fox_proj(x, mix_norm[i], fox_w_in[j], fox_b_f[j])
            mixer = (_fox_attn(ka, qt, vt).reshape(n, d), fox_w_out[j])
        else:
            x = _lru(x, mix_norm[i], lru_w_in[j], lru_conv_w[j], lru_conv_b[j], lru_w_a[j],
                     lru_b_a[j], lru_w_i[j], lru_b_i[j], lru_lambda[j], lru_w_out[j])
        x = _ffn(x.reshape(n, d), ffn2_norm[i], ffn2_w_gu, ffn2_w_down, i, final_norm,
                 i == DEPTH - 1, mixer).reshape(b, t, d)
    return x
```

```python
import functools

import jax
import jax.numpy as jnp
import numpy as np
from jax import lax
from jax.experimental import pallas as pl
from jax.experimental.pallas import tpu as pltpu

D_MODEL = 1024
DEPTH = 2
N_MIXERS = 2
FOX_HEADS = 16
FOX_HEAD_DIM = D_MODEL // FOX_HEADS
RNN_WIDTH = 1280
RNN_BLOCKS = 16
RNN_BLOCK_W = RNN_WIDTH // RNN_BLOCKS
CONV_WIDTH = 4
RG_C = 8.0
D_FF = 2816
RMS_EPS = 1e-6
LOG2E = float(np.log2(np.e))

LANES = 128
SUBLANES = 8
BF16_SUBLANES = 16
VMEM_LIMIT_BYTES = 56 * 1024 * 1024

FFN_ROWS = 1024
FFN_SUB_ROWS = 512
FFN_STAGE_STEPS = 8
MXU_TILE = 256
FFN_CHUNK_BOUNDS = (0, 6 * MXU_TILE, D_FF)
PROJ_ROWS = 512
ATTN_BLOCK = 512
ATTN_KEY_BLOCKS = 2
ATTN_UNIT_KEYS = 256
ATTN_LOOKAHEAD = 2
LRU_ROWS = 512
LRU_SUB_ROWS = 256
LRU_GROUPS = 2
LRU_GROUP_W = RNN_WIDTH // LRU_GROUPS

HEAD_K = LANES
AUG_SLOTS = SUBLANES
F_PIECES = 3
V_ROWS = FOX_HEAD_DIM + BF16_SUBLANES

BF16 = jnp.bfloat16
F32 = jnp.float32


def _params(*sem):
    return pltpu.CompilerParams(dimension_semantics=sem, vmem_limit_bytes=VMEM_LIMIT_BYTES)


def _resident(shape):
    nd = len(shape)
    return pl.BlockSpec(shape, lambda *_: (0,) * nd, pipeline_mode=pl.Buffered(1))


def _rmsnorm(x, g):
    ms = jnp.mean(x * x, axis=-1, keepdims=True)
    return x * lax.rsqrt(ms + RMS_EPS) * g


def _dot(a, b):
    return jnp.dot(a, b, preferred_element_type=F32)


def _sigmoid(x):
    return 1.0 / (1.0 + jnp.exp2(x * (-LOG2E)))


def _ffn_kernel(*refs, final_norm, mixer_proj):
    if mixer_proj:
        x_ref, m_ref, mw_ref, g_ref, wgu32_ref, wd32_ref, fg_ref, o_ref, wgu_ref, wd_ref = refs
    else:
        x_ref, g_ref, wgu32_ref, wd32_ref, fg_ref, o_ref, wgu_ref, wd_ref = refs
        m_ref = mw_ref = None
    step = pl.program_id(0)

    @pl.when(step < FFN_STAGE_STEPS)
    def _():
        gu_rows, d_rows = wgu32_ref.shape[0], wd32_ref.shape[0]
        wgu_ref[pl.ds(pl.multiple_of(step * gu_rows, gu_rows), gu_rows), :] = (
            wgu32_ref[...].astype(BF16))
        wd_ref[pl.ds(pl.multiple_of(step * d_rows, d_rows), d_rows), :] = (
            wd32_ref[...].astype(BF16))

    @pl.when(step >= FFN_STAGE_STEPS)
    def _():
        _ffn_tile(x_ref, m_ref, mw_ref, g_ref, wgu_ref, wd_ref, fg_ref, o_ref, final_norm)


def _ffn_tile(x_ref, m_ref, mw_ref, g_ref, wgu_ref, wd_ref, fg_ref, o_ref, final_norm):
    chunks = list(zip(FFN_CHUNK_BOUNDS[:-1], FFN_CHUNK_BOUNDS[1:]))
    n_sub = FFN_ROWS // FFN_SUB_ROWS

    def rows(i):
        return slice(i * FFN_SUB_ROWS, (i + 1) * FFN_SUB_ROWS)

    def normed(i):
        x = x_ref[rows(i), :]
        if m_ref is not None:
            x = x + _dot(m_ref[rows(i), :], mw_ref[...])
            o_ref[rows(i), :] = x
        return _rmsnorm(x, g_ref[...]).astype(BF16)

    def finish(i, acc):
        base = x_ref if m_ref is None else o_ref
        y = base[rows(i), :] + 0.5 * acc
        if final_norm:
            y = _rmsnorm(y, fg_ref[...])
        o_ref[rows(i), :] = y

    h = normed(0)
    done = None
    for i in range(n_sub):
        acc = None
        for c, (lo, hi) in enumerate(chunks):
            gate = _dot(h, wgu_ref[:, lo:hi])
            if c == 0:
                h_next = normed(i + 1) if i + 1 < n_sub else None
                if done is not None:
                    finish(*done)
            up = _dot(h, wgu_ref[:, D_FF + lo:D_FF + hi])
            a = (gate * _sigmoid(gate) * up).astype(BF16)
            part = _dot(a, wd_ref[lo:hi, :])
            acc = part if acc is None else acc + part
        done = (i, acc)
        h = h_next
    finish(*done)


def _ffn(x2d, g, w_gu_all, w_down_all, layer, final_g, final_norm, mixer=None):
    n = x2d.shape[0]
    stage = FFN_STAGE_STEPS
    last = stage - 1

    def row_map(i):
        return (jnp.maximum(i - stage, 0), 0)

    row = pl.BlockSpec((FFN_ROWS, D_MODEL), row_map)
    operands, specs = [x2d], [row]
    if mixer is not None:
        m2d, w_m = mixer
        operands += [m2d, w_m.astype(BF16)]
        specs += [pl.BlockSpec((FFN_ROWS, m2d.shape[1]), row_map), _resident(w_m.shape)]
    operands += [g.reshape(1, D_MODEL), w_gu_all, w_down_all, final_g.reshape(1, D_MODEL)]
    specs += [_resident((1, D_MODEL)),
              pl.BlockSpec((None, D_MODEL // stage, 2 * D_FF),
                           lambda i: (layer, jnp.minimum(i, last), 0)),
              pl.BlockSpec((None, D_FF // stage, D_MODEL),
                           lambda i: (layer, jnp.minimum(i, last), 0)),
              _resident((1, D_MODEL))]
    return pl.pallas_call(
        functools.partial(_ffn_kernel, final_norm=final_norm, mixer_proj=mixer is not None),
        grid=(stage + n // FFN_ROWS,),
        in_specs=specs,
        out_specs=row,
        out_shape=jax.ShapeDtypeStruct((n, D_MODEL), F32),
        scratch_shapes=[pltpu.VMEM((D_MODEL, 2 * D_FF), BF16),
                        pltpu.VMEM((D_FF, D_MODEL), BF16)],
        compiler_params=_params("arbitrary"),
        name="ffn",
    )(*operands)


def _log_sigmoid(z):
    return jnp.minimum(z, 0.0) - jnp.log1p(jnp.exp(-jnp.abs(z)))


def _cumsum_lanes(v):
    n = v.shape[-1]
    lane = lax.broadcasted_iota(jnp.int32, v.shape, v.ndim - 1)
    s = 1
    while s < n:
        v = v + jnp.where(lane >= s, pltpu.roll(v, s, v.ndim - 1), 0.0)
        s *= 2
    return v


def _split_bf16_pieces(f):
    p1 = f.astype(BF16).astype(F32)
    r1 = f - p1
    p2 = r1.astype(BF16).astype(F32)
    return p1, p2, r1 - p2


def _fox_proj_kernel(x_ref, g_ref, win32_ref, place_ref, bf_ref,
                     ka_ref, qt_ref, vt_ref, carry_ref, wqkv_ref, wtail_ref):
    rows = PROJ_ROWS
    dh = FOX_HEAD_DIM
    n_in = win32_ref.shape[0]

    @pl.when((pl.program_id(0) == 0) & (pl.program_id(1) == 0))
    def _():
        for c in range(3 * D_MODEL // LANES):
            wqkv_ref[:, c * LANES:(c + 1) * LANES] = (
                win32_ref[c * LANES:(c + 1) * LANES, :].T.astype(BF16))
        wtail_ref[...] = win32_ref[n_in - LANES:, :].T.astype(BF16)

    @pl.when(pl.program_id(1) == 0)
    def _():
        carry_ref[...] = jnp.zeros_like(carry_ref)

    h = _rmsnorm(x_ref[0], g_ref[...]).astype(BF16)
    f_logit = _dot(h, wtail_ref[...]).T[LANES - FOX_HEADS:, :]
    log_f = _log_sigmoid(f_logit + bf_ref[...])
    cum = _cumsum_lanes(log_f) + carry_ref[:, 0:1]
    carry_ref[...] = jnp.broadcast_to(cum[:, rows - 1:rows], carry_ref.shape)
    pieces = _split_bf16_pieces(cum * LOG2E)

    qkv = _dot(h, wqkv_ref[...])
    sub_q = lax.broadcasted_iota(jnp.int32, (BF16_SUBLANES, rows), 0)
    sub_k = lax.broadcasted_iota(jnp.int32, (AUG_SLOTS, rows), 0)
    vt_tail = jnp.where(sub_q == 0, 1.0, 0.0).astype(BF16)
    q_scale = LOG2E * dh ** -0.5
    ka_aug = []
    for hh in range(FOX_HEADS):
        aug_q = jnp.where(sub_q < 2 * F_PIECES, 1.0, 0.0)
        aug_k = jnp.where(sub_k < F_PIECES, 1.0, 0.0)
        for j in range(F_PIECES):
            pj = pieces[j][hh:hh + 1]
            aug_q = jnp.where(sub_q == j, pj, aug_q)
            aug_k = jnp.where(sub_k == F_PIECES + j, -pj, aug_k)
        qt_ref[0, hh, dh:dh + BF16_SUBLANES, :] = aug_q.astype(BF16)
        qt_ref[0, hh, dh + BF16_SUBLANES:, :] = jnp.zeros(
            (HEAD_K - dh - BF16_SUBLANES, rows), BF16)
        vt_ref[0, hh, dh:, :] = vt_tail
        ka_aug.append(aug_k)
    ka_tok = jnp.concatenate(ka_aug, axis=0).T.astype(BF16)
    aug_wide = _dot(ka_tok, place_ref[...])
    lane = lax.broadcasted_iota(jnp.int32, (rows, LANES), 1)
    for pair in range(FOX_HEADS // 2):
        cols = slice(pair * LANES, (pair + 1) * LANES)
        q_t = (qkv[:, cols] * q_scale).T
        v_t = qkv[:, 2 * D_MODEL + pair * LANES:2 * D_MODEL + (pair + 1) * LANES].T
        k_even = qkv[:, D_MODEL + pair * LANES:D_MODEL + (pair + 1) * LANES]
        k_odd = pltpu.roll(k_even, dh, 1)
        for e, k_src in enumerate((k_even, k_odd)):
            hh = 2 * pair + e
            qt_ref[0, hh, 0:dh, :] = q_t[e * dh:(e + 1) * dh].astype(BF16)
            vt_ref[0, hh, 0:dh, :] = v_t[e * dh:(e + 1) * dh].astype(BF16)
            ka_ref[0, hh] = jnp.where(lane < dh, k_src,
                                      aug_wide[:, hh * HEAD_K:(hh + 1) * HEAD_K]).astype(BF16)


def _aug_placement():
    p = np.zeros((FOX_HEADS * AUG_SLOTS, FOX_HEADS * HEAD_K), np.float32)
    for hh in range(FOX_HEADS):
        for j in range(AUG_SLOTS):
            p[hh * AUG_SLOTS + j, hh * HEAD_K + FOX_HEAD_DIM + j] = 1.0
    return p


def _fox_block_spec(tile, index_map):
    return pl.BlockSpec((None, 1, FOX_HEADS) + tile, lambda *a: index_map(*a) + (0, 0, 0))


def _fox_proj(x, g, w_in, b_f):
    b, t, d = x.shape
    assert PROJ_ROWS == ATTN_BLOCK
    nb = t // PROJ_ROWS
    hk = FOX_HEADS * HEAD_K
    place = jnp.asarray(_aug_placement(), BF16)
    row = pl.BlockSpec((1, PROJ_ROWS, d), lambda bi, ti: (bi, ti, 0))
    return pl.pallas_call(
        _fox_proj_kernel,
        grid=(b, nb),
        in_specs=[row, _resident((1, d)), _resident(w_in.shape[::-1]),
                  _resident((FOX_HEADS * AUG_SLOTS, hk)), _resident((FOX_HEADS, 1))],
        out_specs=[_fox_block_spec((PROJ_ROWS, HEAD_K), lambda bi, ti: (bi, ti)),
                   _fox_block_spec((HEAD_K, PROJ_ROWS), lambda bi, ti: (bi, ti)),
                   _fox_block_spec((V_ROWS, PROJ_ROWS), lambda bi, ti: (bi, ti))],
        out_shape=[jax.ShapeDtypeStruct((b, nb, FOX_HEADS, PROJ_ROWS, HEAD_K), BF16),
                   jax.ShapeDtypeStruct((b, nb, FOX_HEADS, HEAD_K, PROJ_ROWS), BF16),
                   jax.ShapeDtypeStruct((b, nb, FOX_HEADS, V_ROWS, PROJ_ROWS), BF16)],
        scratch_shapes=[pltpu.VMEM((FOX_HEADS, LANES), F32), pltpu.VMEM((d, 3 * d), BF16),
                        pltpu.VMEM((d, LANES), BF16)],
        compiler_params=_params("arbitrary", "arbitrary"),
        name="fox_proj",
    )(x, g.reshape(1, d), w_in.T, place, b_f.reshape(FOX_HEADS, 1))


def _fox_attn_kernel(qi_ref, kj_ref, ka_ref, qt_ref, vt_ref, o_ref, m_ref, acc_ref):
    step = pl.program_id(1)
    qi = qi_ref[step]
    first = kj_ref[step] * ATTN_KEY_BLOCKS
    blk = ATTN_BLOCK
    dh = FOX_HEAD_DIM

    @pl.when(first == 0)
    def _():
        m_ref[...] = jnp.full_like(m_ref, -jnp.inf)
        acc_ref[...] = jnp.zeros_like(acc_ref)

    def run(n_blocks, diag_last):
        uk = ATTN_UNIT_KEYS
        parts = blk // uk
        if diag_last:
            key = lax.broadcasted_iota(jnp.int32, (uk, blk), 0)
            qry = lax.broadcasted_iota(jnp.int32, (uk, blk), 1)

        def split(u):
            jp, h = divmod(u, FOX_HEADS)
            j, part = divmod(jp, parts)
            return j, part, h

        def scores(u):
            j, part, h = split(u)
            s = _dot(ka_ref[j, h, part * uk:(part + 1) * uk, :], qt_ref[0, h])
            if diag_last and j == n_blocks - 1:
                s = jnp.where(key + part * uk <= qry, s, -jnp.inf)
            m_prev = m_ref[h]
            m_new = jnp.maximum(m_prev, jnp.max(s, axis=0, keepdims=True))
            m_ref[h] = m_new
            return s, m_new, jnp.exp2(m_prev - m_new)

        units = n_blocks * parts * FOX_HEADS
        ahead = [scores(u) for u in range(ATTN_LOOKAHEAD)]
        for u in range(units):
            j, part, h = split(u)
            s, m_new, alpha = ahead.pop(0)
            if u + ATTN_LOOKAHEAD < units:
                ahead.append(scores(u + ATTN_LOOKAHEAD))
            p = jnp.exp2(s - m_new).astype(BF16)
            pv = _dot(vt_ref[j, h, :, part * uk:(part + 1) * uk], p)
            acc_ref[h] = alpha * acc_ref[h] + pv

    def finalize():
        for hp in range(FOX_HEADS // 2):
            pair = []
            for h in (2 * hp, 2 * hp + 1):
                a = acc_ref[h]
                pair.append(a[0:dh] * (1.0 / a[dh:dh + 1]))
            o_ref[0, :, hp * LANES:(hp + 1) * LANES] = (
                jnp.concatenate(pair, axis=0).T.astype(BF16))

    @pl.when(first + ATTN_KEY_BLOCKS - 1 < qi)
    def _():
        run(ATTN_KEY_BLOCKS, False)

    for n in range(1, ATTN_KEY_BLOCKS + 1):
        @pl.when(first + n - 1 == qi)
        def _(n=n):
            run(n, True)
            finalize()


def _fox_attn(ka, qt, vt):
    b, nb = ka.shape[:2]
    t = nb * ATTN_BLOCK
    assert nb % ATTN_KEY_BLOCKS == 0
    groups = [-(-(i + 1) // ATTN_KEY_BLOCKS) for i in range(nb)]
    qi_tab = np.concatenate([np.full(g, i) for i, g in enumerate(groups)]).astype(np.int32)
    kj_tab = np.concatenate([np.arange(g) for g in groups]).astype(np.int32)
    o_spec = pl.BlockSpec((1, ATTN_BLOCK, D_MODEL), lambda bi, s, qi, kj: (bi, qi[s], 0))

    def key_spec(tile):
        return pl.BlockSpec((None, ATTN_KEY_BLOCKS, FOX_HEADS) + tile,
                            lambda bi, s, qi, kj: (bi, kj[s], 0, 0, 0))

    grid_spec = pltpu.PrefetchScalarGridSpec(
        num_scalar_prefetch=2,
        grid=(b, len(qi_tab)),
        in_specs=[key_spec((ATTN_BLOCK, HEAD_K)),
                  _fox_block_spec((HEAD_K, ATTN_BLOCK), lambda bi, s, qi, kj: (bi, qi[s])),
                  key_spec((V_ROWS, ATTN_BLOCK))],
        out_specs=o_spec,
        scratch_shapes=[pltpu.VMEM((FOX_HEADS, 1, ATTN_BLOCK), F32),
                        pltpu.VMEM((FOX_HEADS, V_ROWS, ATTN_BLOCK), F32)],
    )
    return pl.pallas_call(
        _fox_attn_kernel,
        grid_spec=grid_spec,
        out_shape=jax.ShapeDtypeStruct((b, t, D_MODEL), BF16),
        compiler_params=_params("arbitrary", "arbitrary"),
        name="fox_attn",
    )(jnp.asarray(qi_tab), jnp.asarray(kj_tab), ka, qt, vt)


def _gelu_tanh(x):
    k1 = float(2.0 * np.sqrt(2.0 / np.pi))
    return x * _sigmoid(x * (k1 + (k1 * 0.044715) * (x * x)))


def _softplus(z):
    return jnp.maximum(z, 0.0) + jnp.log1p(jnp.exp(-jnp.abs(z)))


def _linear_scan_rows(a, b, h0):
    rows, n = a.shape
    groups = rows // SUBLANES
    a = a.reshape(groups, SUBLANES, n)
    b = b.reshape(groups, SUBLANES, n)
    sub = lax.broadcasted_iota(jnp.int32, a.shape, 1)
    s = 1
    while s < SUBLANES:
        keep = sub >= s
        b = jnp.where(keep, a * pltpu.roll(b, s, 1) + b, b)
        a = jnp.where(keep, a * pltpu.roll(a, s, 1), a)
        s *= 2
    out = []
    carry = h0
    for g in range(groups):
        hg = b[g] + a[g] * carry
        out.append(hg)
        carry = hg[SUBLANES - 1:SUBLANES, :]
    return jnp.concatenate(out, axis=0)


def _lru_kernel(x_ref, g_ref, win32_ref, cw_ref, cb_ref, wg_ref, ba_ref, bi_ref, lam_ref,
                wout32_ref, y_ref, ext_ref, h_ref, win_ref, wout_ref):
    rows = LRU_ROWS
    r = RNN_WIDTH

    @pl.when((pl.program_id(0) == 0) & (pl.program_id(1) == 0))
    def _():
        win_ref[...] = win32_ref[...].astype(BF16)
        wout_ref[...] = wout32_ref[...].astype(BF16)

    @pl.when(pl.program_id(1) == 0)
    def _():
        ext_ref[...] = jnp.zeros_like(ext_ref)
        h_ref[...] = jnp.zeros_like(h_ref)

    gw = LRU_GROUP_W
    sub = LRU_SUB_ROWS
    n_sub = rows // sub
    groups = [slice(c * gw, (c + 1) * gw) for c in range(LRU_GROUPS)]

    def tile(i):
        return slice(i * sub, (i + 1) * sub)

    def normed(i):
        return _rmsnorm(x_ref[0, tile(i), :], g_ref[...]).astype(BF16)

    def conv(rec_all):
        out = []
        for lanes in groups:
            rec = rec_all[:, lanes]
            ext = jnp.concatenate([ext_ref[:, lanes], rec], axis=0)
            ext_ref[:, lanes] = rec[sub - SUBLANES:, :]
            xc = cb_ref[:, lanes]
            for j in range(CONV_WIDTH):
                lag = CONV_WIDTH - 1 - j
                past = pltpu.roll(ext, lag, 0)[SUBLANES:, :] if lag else rec
                xc = xc + past * cw_ref[j:j + 1, lanes]
            out.append(xc)
        return out

    def scan(xcs, boths, gate_all):
        ys = []
        for lanes, xc, both in zip(groups, xcs, boths):
            rg = _sigmoid(both[:, :gw] + ba_ref[:, lanes])
            ig = _sigmoid(both[:, gw:] + bi_ref[:, lanes])
            log_a = (-RG_C * _softplus(-lam_ref[:, lanes])) * rg
            a = jnp.exp(log_a)
            th = jnp.tanh(log_a)
            w = -2.0 * th / (1.0 - th)
            bb = jnp.where(w > 0.0, w * lax.rsqrt(w), 0.0) * (ig * xc)
            hs = _linear_scan_rows(a, bb, h_ref[0:1, lanes])
            h_ref[:, lanes] = jnp.broadcast_to(hs[sub - 1:sub, :], (SUBLANES, gw))
            ys.append((hs * _gelu_tanh(gate_all[:, lanes])).astype(BF16))
        return jnp.concatenate(ys, axis=1)

    h = normed(0)
    rec_all = _dot(h, win_ref[:, r:])
    gate_all = _dot(h, win_ref[:, :r])
    for i in range(n_sub):
        more = i + 1 < n_sub
        if more:
            h = normed(i + 1)
            rec_next = _dot(h, win_ref[:, r:])
        xcs = conv(rec_all)
        boths = [_dot(xc.astype(BF16), wg_ref[c]) for c, xc in enumerate(xcs)]
        if more:
            gate_next = _dot(h, win_ref[:, :r])
        y = scan(xcs, boths, gate_all)
        y_ref[0, tile(i), :] = x_ref[0, tile(i), :] + _dot(y, wout_ref[...])
        if more:
            rec_all, gate_all = rec_next, gate_next


def _block_diag_groups(w):
    per = RNN_BLOCKS // LRU_GROUPS
    w = w.reshape(LRU_GROUPS, per, RNN_BLOCK_W, 1, RNN_BLOCK_W)
    on_diag = jnp.eye(per, dtype=bool).reshape(1, per, 1, per, 1)
    return jnp.where(on_diag, w, 0.0).reshape(LRU_GROUPS, LRU_GROUP_W, LRU_GROUP_W)


def _lru(x, g, w_in, conv_w, conv_b, w_a, b_a, w_i, b_i, lam, w_out):
    b, t, _ = x.shape
    r = RNN_WIDTH
    w_gates = jnp.concatenate([_block_diag_groups(w_a), _block_diag_groups(w_i)],
                              axis=2).astype(BF16)
    row = pl.BlockSpec((1, LRU_ROWS, D_MODEL), lambda bi, ti: (bi, ti, 0))
    vec = _resident((1, r))
    return pl.pallas_call(
        _lru_kernel,
        grid=(b, t // LRU_ROWS),
        in_specs=[row, _resident((1, D_MODEL)), _resident((D_MODEL, 2 * r)),
                  _resident((CONV_WIDTH, r)), vec,
                  _resident((LRU_GROUPS, LRU_GROUP_W, 2 * LRU_GROUP_W)), vec, vec, vec,
                  _resident((r, D_MODEL))],
        out_specs=row,
        out_shape=jax.ShapeDtypeStruct((b, t, D_MODEL), F32),
        scratch_shapes=[pltpu.VMEM((SUBLANES, r), F32),
                        pltpu.VMEM((SUBLANES, r), F32),
                        pltpu.VMEM((D_MODEL, 2 * r), BF16),
                        pltpu.VMEM((r, D_MODEL), BF16)],
        compiler_params=_params("arbitrary", "arbitrary"),
        name="rglru",
    )(x, g.reshape(1, D_MODEL), w_in, conv_w, conv_b.reshape(1, r), w_gates,
      b_a.reshape(1, r), b_i.reshape(1, r), lam.reshape(1, r), w_out)


def kernel(x, ffn1_norm, ffn1_w_gu, ffn1_w_down, mix_norm, ffn2_norm, ffn2_w_gu, ffn2_w_down,
           fox_w_in, fox_b_f, fox_w_out, lru_w_in, lru_conv_w, lru_conv_b, lru_w_a, lru_b_a,
           lru_w_i, lru_b_i, lru_lambda, lru_w_out, final_norm):
    b, t, d = x.shape
    n = b * t
    for i in range(DEPTH):
        x = _ffn(x.reshape(n, d), ffn1_norm[i], ffn1_w_gu, ffn1_w_down, i, final_norm,
                 False).reshape(b, t, d)
        j = i // N_MIXERS
        mixer = None
        if i % N_MIXERS == 0:
            ka, qt, vt = _fox_proj(x, mix_norm[i], fox_w_in[j], fox_b_f[j])
            mixer = (_fox_attn(ka, qt, vt).reshape(n, d), fox_w_out[j])
        else:
            x = _lru(x, mix_norm[i], lru_w_in[j], lru_conv_w[j], lru_conv_b[j], lru_w_a[j],
                     lru_b_a[j], lru_w_i[j], lru_b_i[j], lru_lambda[j], lru_w_out[j])
        x = _ffn(x.reshape(n, d), ffn2_norm[i], ffn2_w_gu, ffn2_w_down, i, final_norm,
                 i == DEPTH - 1, mixer).reshape(b, t, d)
    return x
```

```python
import functools

import jax
import jax.numpy as jnp
import numpy as np
from jax import lax
from jax.experimental import pallas as pl
from jax.experimental.pallas import tpu as pltpu

D_MODEL = 1024
DEPTH = 2
N_MIXERS = 2
FOX_HEADS = 16
FOX_HEAD_DIM = D_MODEL // FOX_HEADS
RNN_WIDTH = 1280
RNN_BLOCKS = 16
RNN_BLOCK_W = RNN_WIDTH // RNN_BLOCKS
CONV_WIDTH = 4
RG_C = 8.0
D_FF = 2816
RMS_EPS = 1e-6
LOG2E = float(np.log2(np.e))

LANES = 128
SUBLANES = 8
BF16_SUBLANES = 16
VMEM_LIMIT_BYTES = 56 * 1024 * 1024

FFN_ROWS = 1024
FFN_SUB_ROWS = 512
FFN_STAGE_STEPS = 8
MXU_TILE = 256
FFN_CHUNK_BOUNDS = (0, 6 * MXU_TILE, D_FF)
PROJ_ROWS = 512
ATTN_BLOCK = 512
ATTN_KEY_BLOCKS = 2
ATTN_UNIT_KEYS = 256
ATTN_LOOKAHEAD = 2
LRU_ROWS = 1024
LRU_SUB_ROWS = 256
LRU_GROUPS = 2
LRU_GROUP_W = RNN_WIDTH // LRU_GROUPS

HEAD_K = LANES
AUG_SLOTS = SUBLANES
F_PIECES = 3
V_ROWS = FOX_HEAD_DIM + BF16_SUBLANES

BF16 = jnp.bfloat16
F32 = jnp.float32


def _params(*sem):
    return pltpu.CompilerParams(dimension_semantics=sem, vmem_limit_bytes=VMEM_LIMIT_BYTES)


def _resident(shape):
    nd = len(shape)
    return pl.BlockSpec(shape, lambda *_: (0,) * nd, pipeline_mode=pl.Buffered(1))


def _rmsnorm(x, g):
    ms = jnp.mean(x * x, axis=-1, keepdims=True)
    return x * lax.rsqrt(ms + RMS_EPS) * g


def _dot(a, b):
    return jnp.dot(a, b, preferred_element_type=F32)


def _sigmoid(x):
    return 1.0 / (1.0 + jnp.exp2(x * (-LOG2E)))


def _ffn_kernel(*refs, final_norm, mixer_proj):
    if mixer_proj:
        x_ref, m_ref, mw_ref, g_ref, wgu32_ref, wd32_ref, fg_ref, o_ref, wgu_ref, wd_ref = refs
    else:
        x_ref, g_ref, wgu32_ref, wd32_ref, fg_ref, o_ref, wgu_ref, wd_ref = refs
        m_ref = mw_ref = None
    step = pl.program_id(0)

    @pl.when(step < FFN_STAGE_STEPS)
    def _():
        gu_rows, d_rows = wgu32_ref.shape[0], wd32_ref.shape[0]
        wgu_ref[pl.ds(pl.multiple_of(step * gu_rows, gu_rows), gu_rows), :] = (
            wgu32_ref[...].astype(BF16))
        wd_ref[pl.ds(pl.multiple_of(step * d_rows, d_rows), d_rows), :] = (
            wd32_ref[...].astype(BF16))

    @pl.when(step >= FFN_STAGE_STEPS)
    def _():
        _ffn_tile(x_ref, m_ref, mw_ref, g_ref, wgu_ref, wd_ref, fg_ref, o_ref, final_norm)


def _ffn_tile(x_ref, m_ref, mw_ref, g_ref, wgu_ref, wd_ref, fg_ref, o_ref, final_norm):
    chunks = list(zip(FFN_CHUNK_BOUNDS[:-1], FFN_CHUNK_BOUNDS[1:]))
    n_sub = FFN_ROWS // FFN_SUB_ROWS

    def rows(i):
        return slice(i * FFN_SUB_ROWS, (i + 1) * FFN_SUB_ROWS)

    def normed(i):
        x = x_ref[rows(i), :]
        if m_ref is not None:
            x = x + _dot(m_ref[rows(i), :], mw_ref[...])
            o_ref[rows(i), :] = x
        return _rmsnorm(x, g_ref[...]).astype(BF16)

    def finish(i, acc):
        base = x_ref if m_ref is None else o_ref
        y = base[rows(i), :] + 0.5 * acc
        if final_norm:
            y = _rmsnorm(y, fg_ref[...])
        o_ref[rows(i), :] = y

    h = normed(0)
    done = None
    for i in range(n_sub):
        acc = None
        for c, (lo, hi) in enumerate(chunks):
            gate = _dot(h, wgu_ref[:, lo:hi])
            if c == 0:
                h_next = normed(i + 1) if i + 1 < n_sub else None
                if done is not None:
                    finish(*done)
            up = _dot(h, wgu_ref[:, D_FF + lo:D_FF + hi])
            a = (gate * _sigmoid(gate) * up).astype(BF16)
            part = _dot(a, wd_ref[lo:hi, :])
            acc = part if acc is None else acc + part
        done = (i, acc)
        h = h_next
    finish(*done)


def _ffn(x2d, g, w_gu_all, w_down_all, layer, final_g, final_norm, mixer=None):
    n = x2d.shape[0]
    stage = FFN_STAGE_STEPS
    last = stage - 1

    def row_map(i):
        return (jnp.maximum(i - stage, 0), 0)

    row = pl.BlockSpec((FFN_ROWS, D_MODEL), row_map)
    operands, specs = [x2d], [row]
    if mixer is not None:
        m2d, w_m = mixer
        operands += [m2d, w_m.astype(BF16)]
        specs += [pl.BlockSpec((FFN_ROWS, m2d.shape[1]), row_map), _resident(w_m.shape)]
    operands += [g.reshape(1, D_MODEL), w_gu_all, w_down_all, final_g.reshape(1, D_MODEL)]
    specs += [_resident((1, D_MODEL)),
              pl.BlockSpec((None, D_MODEL // stage, 2 * D_FF),
                           lambda i: (layer, jnp.minimum(i, last), 0)),
              pl.BlockSpec((None, D_FF // stage, D_MODEL),
                           lambda i: (layer, jnp.minimum(i, last), 0)),
              _resident((1, D_MODEL))]
    return pl.pallas_call(
        functools.partial(_ffn_kernel, final_norm=final_norm, mixer_proj=mixer is not None),
        grid=(stage + n // FFN_ROWS,),
        in_specs=specs,
        out_specs=row,
        out_shape=jax.ShapeDtypeStruct((n, D_MODEL), F32),
        scratch_shapes=[pltpu.VMEM((D_MODEL, 2 * D_FF), BF16),
                        pltpu.VMEM((D_FF, D_MODEL), BF16)],
        compiler_params=_params("arbitrary"),
        name="ffn",
    )(*operands)


def _log_sigmoid(z):
    return jnp.minimum(z, 0.0) - jnp.log1p(jnp.exp(-jnp.abs(z)))


def _cumsum_lanes(v):
    n = v.shape[-1]
    lane = lax.broadcasted_iota(jnp.int32, v.shape, v.ndim - 1)
    s = 1
    while s < n:
        v = v + jnp.where(lane >= s, pltpu.roll(v, s, v.ndim - 1), 0.0)
        s *= 2
    return v


def _split_bf16_pieces(f):
    p1 = f.astype(BF16).astype(F32)
    r1 = f - p1
    p2 = r1.astype(BF16).astype(F32)
    return p1, p2, r1 - p2


def _fox_proj_kernel(x_ref, g_ref, win32_ref, place_ref, bf_ref,
                     ka_ref, qt_ref, vt_ref, carry_ref, wqkv_ref, wtail_ref):
    rows = PROJ_ROWS
    dh = FOX_HEAD_DIM
    n_in = win32_ref.shape[0]

    @pl.when((pl.program_id(0) == 0) & (pl.program_id(1) == 0))
    def _():
        for c in range(3 * D_MODEL // LANES):
            wqkv_ref[:, c * LANES:(c + 1) * LANES] = (
                win32_ref[c * LANES:(c + 1) * LANES, :].T.astype(BF16))
        wtail_ref[...] = win32_ref[n_in - LANES:, :].T.astype(BF16)

    @pl.when(pl.program_id(1) == 0)
    def _():
        carry_ref[...] = jnp.zeros_like(carry_ref)

    h = _rmsnorm(x_ref[0], g_ref[...]).astype(BF16)
    f_logit = _dot(h, wtail_ref[...]).T[LANES - FOX_HEADS:, :]
    log_f = _log_sigmoid(f_logit + bf_ref[...])
    cum = _cumsum_lanes(log_f) + carry_ref[:, 0:1]
    carry_ref[...] = jnp.broadcast_to(cum[:, rows - 1:rows], carry_ref.shape)
    pieces = _split_bf16_pieces(cum * LOG2E)

    qkv = _dot(h, wqkv_ref[...])
    sub_q = lax.broadcasted_iota(jnp.int32, (BF16_SUBLANES, rows), 0)
    sub_k = lax.broadcasted_iota(jnp.int32, (AUG_SLOTS, rows), 0)
    vt_tail = jnp.where(sub_q == 0, 1.0, 0.0).astype(BF16)
    q_scale = LOG2E * dh ** -0.5
    ka_aug = []
    for hh in range(FOX_HEADS):
        aug_q = jnp.where(sub_q < 2 * F_PIECES, 1.0, 0.0)
        aug_k = jnp.where(sub_k < F_PIECES, 1.0, 0.0)
        for j in range(F_PIECES):
            pj = pieces[j][hh:hh + 1]
            aug_q = jnp.where(sub_q == j, pj, aug_q)
            aug_k = jnp.where(sub_k == F_PIECES + j, -pj, aug_k)
        qt_ref[0, hh, dh:dh + BF16_SUBLANES, :] = aug_q.astype(BF16)
        qt_ref[0, hh, dh + BF16_SUBLANES:, :] = jnp.zeros(
            (HEAD_K - dh - BF16_SUBLANES, rows), BF16)
        vt_ref[0, hh, dh:, :] = vt_tail
        ka_aug.append(aug_k)
    ka_tok = jnp.concatenate(ka_aug, axis=0).T.astype(BF16)
    aug_wide = _dot(ka_tok, place_ref[...])
    lane = lax.broadcasted_iota(jnp.int32, (rows, LANES), 1)
    for pair in range(FOX_HEADS // 2):
        cols = slice(pair * LANES, (pair + 1) * LANES)
        q_t = (qkv[:, cols] * q_scale).T
        v_t = qkv[:, 2 * D_MODEL + pair * LANES:2 * D_MODEL + (pair + 1) * LANES].T
        k_even = qkv[:, D_MODEL + pair * LANES:D_MODEL + (pair + 1) * LANES]
        k_odd = pltpu.roll(k_even, dh, 1)
        for e, k_src in enumerate((k_even, k_odd)):
            hh = 2 * pair + e
            qt_ref[0, hh, 0:dh, :] = q_t[e * dh:(e + 1) * dh].astype(BF16)
            vt_ref[0, hh, 0:dh, :] = v_t[e * dh:(e + 1) * dh].astype(BF16)
            ka_ref[0, hh] = jnp.where(lane < dh, k_src,
                                      aug_wide[:, hh * HEAD_K:(hh + 1) * HEAD_K]).astype(BF16)


def _aug_placement():
    p = np.zeros((FOX_HEADS * AUG_SLOTS, FOX_HEADS * HEAD_K), np.float32)
    for hh in range(FOX_HEADS):
        for j in range(AUG_SLOTS):
            p[hh * AUG_SLOTS + j, hh * HEAD_K + FOX_HEAD_DIM + j] = 1.0
    return p


def _fox_block_spec(tile, index_map):
    return pl.BlockSpec((None, 1, FOX_HEADS) + tile, lambda *a: index_map(*a) + (0, 0, 0))


def _fox_proj(x, g, w_in, b_f):
    b, t, d = x.shape
    assert PROJ_ROWS == ATTN_BLOCK
    nb = t // PROJ_ROWS
    hk = FOX_HEADS * HEAD_K
    place = jnp.asarray(_aug_placement(), BF16)
    row = pl.BlockSpec((1, PROJ_ROWS, d), lambda bi, ti: (bi, ti, 0))
    return pl.pallas_call(
        _fox_proj_kernel,
        grid=(b, nb),
        in_specs=[row, _resident((1, d)), _resident(w_in.shape[::-1]),
                  _resident((FOX_HEADS * AUG_SLOTS, hk)), _resident((FOX_HEADS, 1))],
        out_specs=[_fox_block_spec((PROJ_ROWS, HEAD_K), lambda bi, ti: (bi, ti)),
                   _fox_block_spec((HEAD_K, PROJ_ROWS), lambda bi, ti: (bi, ti)),
                   _fox_block_spec((V_ROWS, PROJ_ROWS), lambda bi, ti: (bi, ti))],
        out_shape=[jax.ShapeDtypeStruct((b, nb, FOX_HEADS, PROJ_ROWS, HEAD_K), BF16),
                   jax.ShapeDtypeStruct((b, nb, FOX_HEADS, HEAD_K, PROJ_ROWS), BF16),
                   jax.ShapeDtypeStruct((b, nb, FOX_HEADS, V_ROWS, PROJ_ROWS), BF16)],
        scratch_shapes=[pltpu.VMEM((FOX_HEADS, LANES), F32), pltpu.VMEM((d, 3 * d), BF16),
                        pltpu.VMEM((d, LANES), BF16)],
        compiler_params=_params("arbitrary", "arbitrary"),
        name="fox_proj",
    )(x, g.reshape(1, d), w_in.T, place, b_f.reshape(FOX_HEADS, 1))


def _fox_attn_kernel(qi_ref, kj_ref, ka_ref, qt_ref, vt_ref, o_ref, m_ref, acc_ref):
    step = pl.program_id(1)
    qi = qi_ref[step]
    first = kj_ref[step] * ATTN_KEY_BLOCKS
    blk = ATTN_BLOCK
    dh = FOX_HEAD_DIM

    @pl.when(first == 0)
    def _():
        m_ref[...] = jnp.full_like(m_ref, -jnp.inf)
        acc_ref[...] = jnp.zeros_like(acc_ref)

    def run(n_blocks, diag_last):
        uk = ATTN_UNIT_KEYS
        parts = blk // uk
        if diag_last:
            key = lax.broadcasted_iota(jnp.int32, (uk, blk), 0)
            qry = lax.broadcasted_iota(jnp.int32, (uk, blk), 1)

        def split(u):
            jp, h = divmod(u, FOX_HEADS)
            j, part = divmod(jp, parts)
            return j, part, h

        def scores(u):
            j, part, h = split(u)
            s = _dot(ka_ref[j, h, part * uk:(part + 1) * uk, :], qt_ref[0, h])
            if diag_last and j == n_blocks - 1:
                s = jnp.where(key + part * uk <= qry, s, -jnp.inf)
            m_prev = m_ref[h]
            m_new = jnp.maximum(m_prev, jnp.max(s, axis=0, keepdims=True))
            m_ref[h] = m_new
            return s, m_new, jnp.exp2(m_prev - m_new)

        units = n_blocks * parts * FOX_HEADS
        ahead = [scores(u) for u in range(ATTN_LOOKAHEAD)]
        for u in range(units):
            j, part, h = split(u)
            s, m_new, alpha = ahead.pop(0)
            if u + ATTN_LOOKAHEAD < units:
                ahead.append(scores(u + ATTN_LOOKAHEAD))
            p = jnp.exp2(s - m_new).astype(BF16)
            pv = _dot(vt_ref[j, h, :, part * uk:(part + 1) * uk], p)
            acc_ref[h] = alpha * acc_ref[h] + pv

    def finalize():
        for hp in range(FOX_HEADS // 2):
            pair = []
            for h in (2 * hp, 2 * hp + 1):
                a = acc_ref[h]
                pair.append(a[0:dh] * (1.0 / a[dh:dh + 1]))
            o_ref[0, :, hp * LANES:(hp + 1) * LANES] = (
                jnp.concatenate(pair, axis=0).T.astype(BF16))

    @pl.when(first + ATTN_KEY_BLOCKS - 1 < qi)
    def _():
        run(ATTN_KEY_BLOCKS, False)

    for n in range(1, ATTN_KEY_BLOCKS + 1):
        @pl.when(first + n - 1 == qi)
        def _(n=n):
            run(n, True)
            finalize()


def _fox_attn(ka, qt, vt):
    b, nb = ka.shape[:2]
    t = nb * ATTN_BLOCK
    assert nb % ATTN_KEY_BLOCKS == 0
    groups = [-(-(i + 1) // ATTN_KEY_BLOCKS) for i in range(nb)]
    qi_tab = np.concatenate([np.full(g, i) for i, g in enumerate(groups)]).astype(np.int32)
    kj_tab = np.concatenate([np.arange(g) for g in groups]).astype(np.int32)
    o_spec = pl.BlockSpec((1, ATTN_BLOCK, D_MODEL), lambda bi, s, qi, kj: (bi, qi[s], 0))

    def key_spec(tile):
        return pl.BlockSpec((None, ATTN_KEY_BLOCKS, FOX_HEADS) + tile,
                            lambda bi, s, qi, kj: (bi, kj[s], 0, 0, 0))

    grid_spec = pltpu.PrefetchScalarGridSpec(
        num_scalar_prefetch=2,
        grid=(b, len(qi_tab)),
        in_specs=[key_spec((ATTN_BLOCK, HEAD_K)),
                  _fox_block_spec((HEAD_K, ATTN_BLOCK), lambda bi, s, qi, kj: (bi, qi[s])),
                  key_spec((V_ROWS, ATTN_BLOCK))],
        out_specs=o_spec,
        scratch_shapes=[pltpu.VMEM((FOX_HEADS, 1, ATTN_BLOCK), F32),
                        pltpu.VMEM((FOX_HEADS, V_ROWS, ATTN_BLOCK), F32)],
    )
    return pl.pallas_call(
        _fox_attn_kernel,
        grid_spec=grid_spec,
        out_shape=jax.ShapeDtypeStruct((b, t, D_MODEL), BF16),
        compiler_params=_params("arbitrary", "arbitrary"),
        name="fox_attn",
    )(jnp.asarray(qi_tab), jnp.asarray(kj_tab), ka, qt, vt)


def _gelu_tanh(x):
    k1 = float(2.0 * np.sqrt(2.0 / np.pi))
    return x * _sigmoid(x * (k1 + (k1 * 0.044715) * (x * x)))


def _softplus(z):
    return jnp.maximum(z, 0.0) + jnp.log1p(jnp.exp(-jnp.abs(z)))


def _linear_scan_rows(a, b, h0):
    rows, n = a.shape
    groups = rows // SUBLANES
    a = a.reshape(groups, SUBLANES, n)
    b = b.reshape(groups, SUBLANES, n)
    sub = lax.broadcasted_iota(jnp.int32, a.shape, 1)
    s = 1
    while s < SUBLANES:
        keep = sub >= s
        b = jnp.where(keep, a * pltpu.roll(b, s, 1) + b, b)
        a = jnp.where(keep, a * pltpu.roll(a, s, 1), a)
        s *= 2
    out = []
    carry = h0
    for g in range(groups):
        hg = b[g] + a[g] * carry
        out.append(hg)
        carry = hg[SUBLANES - 1:SUBLANES, :]
    return jnp.concatenate(out, axis=0)


def _lru_kernel(x_ref, g_ref, win32_ref, cw_ref, cb_ref, wg_ref, ba_ref, bi_ref, lam_ref,
                wout32_ref, y_ref, ext_ref, h_ref, win_ref, wout_ref):
    rows = LRU_ROWS
    r = RNN_WIDTH

    @pl.when((pl.program_id(0) == 0) & (pl.program_id(1) == 0))
    def _():
        win_ref[...] = win32_ref[...].astype(BF16)
        wout_ref[...] = wout32_ref[...].astype(BF16)

    @pl.when(pl.program_id(1) == 0)
    def _():
        ext_ref[...] = jnp.zeros_like(ext_ref)
        h_ref[...] = jnp.zeros_like(h_ref)

    gw = LRU_GROUP_W
    sub = LRU_SUB_ROWS
    n_sub = rows // sub
    groups = [slice(c * gw, (c + 1) * gw) for c in range(LRU_GROUPS)]

    def tile(i):
        return slice(i * sub, (i + 1) * sub)

    def normed(i):
        return _rmsnorm(x_ref[0, tile(i), :], g_ref[...]).astype(BF16)

    def conv(rec_all):
        out = []
        for lanes in groups:
            rec = rec_all[:, lanes]
            ext = jnp.concatenate([ext_ref[:, lanes], rec], axis=0)
            ext_ref[:, lanes] = rec[sub - SUBLANES:, :]
            xc = cb_ref[:, lanes]
            for j in range(CONV_WIDTH):
                lag = CONV_WIDTH - 1 - j
                past = pltpu.roll(ext, lag, 0)[SUBLANES:, :] if lag else rec
                xc = xc + past * cw_ref[j:j + 1, lanes]
            out.append(xc)
        return out

    def scan(xcs, boths, gate_all):
        ys = []
        for lanes, xc, both in zip(groups, xcs, boths):
            rg = _sigmoid(both[:, :gw] + ba_ref[:, lanes])
            ig = _sigmoid(both[:, gw:] + bi_ref[:, lanes])
            log_a = (-RG_C * _softplus(-lam_ref[:, lanes])) * rg
            a = jnp.exp(log_a)
            th = jnp.tanh(log_a)
            w = -2.0 * th / (1.0 - th)
            bb = jnp.where(w > 0.0, w * lax.rsqrt(w), 0.0) * (ig * xc)
            hs = _linear_scan_rows(a, bb, h_ref[0:1, lanes])
            h_ref[:, lanes] = jnp.broadcast_to(hs[sub - 1:sub, :], (SUBLANES, gw))
            ys.append((hs * _gelu_tanh(gate_all[:, lanes])).astype(BF16))
        return jnp.concatenate(ys, axis=1)

    h = normed(0)
    rec_all = _dot(h, win_ref[:, r:])
    gate_all = _dot(h, win_ref[:, :r])
    for i in range(n_sub):
        more = i + 1 < n_sub
        if more:
            h = normed(i + 1)
            rec_next = _dot(h, win_ref[:, r:])
        xcs = conv(rec_all)
        boths = [_dot(xc.astype(BF16), wg_ref[c]) for c, xc in enumerate(xcs)]
        if more:
            gate_next = _dot(h, win_ref[:, :r])
        y = scan(xcs, boths, gate_all)
        y_ref[0, tile(i), :] = x_ref[0, tile(i), :] + _dot(y, wout_ref[...])
        if more:
            rec_all, gate_all = rec_next, gate_next


def _block_diag_groups(w):
    per = RNN_BLOCKS // LRU_GROUPS
    w = w.reshape(LRU_GROUPS, per, RNN_BLOCK_W, 1, RNN_BLOCK_W)
    on_diag = jnp.eye(per, dtype=bool).reshape(1, per, 1, per, 1)
    return jnp.where(on_diag, w, 0.0).reshape(LRU_GROUPS, LRU_GROUP_W, LRU_GROUP_W)


def _lru(x, g, w_in, conv_w, conv_b, w_a, b_a, w_i, b_i, lam, w_out):
    b, t, _ = x.shape
    r = RNN_WIDTH
    w_gates = jnp.concatenate([_block_diag_groups(w_a), _block_diag_groups(w_i)],
                              axis=2).astype(BF16)
    row = pl.BlockSpec((1, LRU_ROWS, D_MODEL), lambda bi, ti: (bi, ti, 0))
    vec = _resident((1, r))
    return pl.pallas_call(
        _lru_kernel,
        grid=(b, t // LRU_ROWS),
        in_specs=[row, _resident((1, D_MODEL)), _resident((D_MODEL, 2 * r)),
                  _resident((CONV_WIDTH, r)), vec,
                  _resident((LRU_GROUPS, LRU_GROUP_W, 2 * LRU_GROUP_W)), vec, vec, vec,
                  _resident((r, D_MODEL))],
        out_specs=row,
        out_shape=jax.ShapeDtypeStruct((b, t, D_MODEL), F32),
        scratch_shapes=[pltpu.VMEM((SUBLANES, r), F32),
                        pltpu.VMEM((SUBLANES, r), F32),
                        pltpu.VMEM((D_MODEL, 2 * r), BF16),
                        pltpu.VMEM((r, D_MODEL), BF16)],
        compiler_params=_params("arbitrary", "arbitrary"),
        name="rglru",
    )(x, g.reshape(1, D_MODEL), w_in, conv_w, conv_b.reshape(1, r), w_gates,
      b_a.reshape(1, r), b_i.reshape(1, r), lam.reshape(1, r), w_out)


def kernel(x, ffn1_norm, ffn1_w_gu, ffn1_w_down, mix_norm, ffn2_norm, ffn2_w_gu, ffn2_w_down,
           fox_w_in, fox_b_f, fox_w_out, lru_w_in, lru_conv_w, lru_conv_b, lru_w_a, lru_b_a,
           lru_w_i, lru_b_i, lru_lambda, lru_w_out, final_norm):
    b, t, d = x.shape
    n = b * t
    for i in range(DEPTH):
        x = _ffn(x.reshape(n, d), ffn1_norm[i], ffn1_w_gu, ffn1_w_down, i, final_norm,
                 False).reshape(b, t, d)
        j = i // N_MIXERS
        mixer = None
        if i % N_MIXERS == 0:
            ka, qt, vt = _fox_proj(x, mix_norm[i], fox_w_in[j], fox_b_f[j])
            mixer = (_fox_attn(ka, qt, vt).reshape(n, d), fox_w_out[j])
        else:
            x = _lru(x, mix_norm[i], lru_w_in[j], lru_conv_w[j], lru_conv_b[j], lru_w_a[j],
                     lru_b_a[j], lru_w_i[j], lru_b_i[j], lru_lambda[j], lru_w_out[j])
        x = _ffn(x.reshape(n, d), ffn2_norm[i], ffn2_w_gu, ffn2_w_down, i, final_norm,
                 i == DEPTH - 1, mixer).reshape(b, t, d)
    return x
```
